```python
import jax, jax.numpy as jnp
from jax import lax
import numpy as np

D_MODEL = 4096
BATCH = 2
SEQ = 4096
DEPTH = 1

CONV_WIDTH = 2048
CONV_GROUPS = 16
CONV_K = 3
DN_HEADS = 16
DN_HEAD_DIM = 128
DN_WIDTH = DN_HEADS * DN_HEAD_DIM
DN_CONV_K = 4
CHUNK = 64
D_FF = 11008
EPS = 1e-6
L2_EPS = 1e-6
SPLIT_SIZES = (CONV_WIDTH, CONV_WIDTH, CONV_WIDTH, DN_WIDTH, DN_WIDTH, DN_WIDTH, DN_WIDTH, DN_HEADS, DN_HEADS, D_MODEL, D_MODEL)
SPLIT_POINTS = (2048, 4096, 6144, 8192, 10240, 12288, 14336, 14352, 14368, 18464)
N_IN = 22560

kernel_name = "hybrid_shortconv_gdn_macaron"


def rms_norm(x, g):
    xf = x.astype(jnp.float32)
    y = xf * lax.rsqrt(jnp.mean(xf * xf, axis=-1, keepdims=True) + EPS)
    return (y * g.astype(jnp.float32)).astype(x.dtype)


def l2_normalize(x):
    return x * lax.rsqrt(jnp.sum(x * x, axis=-1, keepdims=True) + L2_EPS)


def swiglu(x, w_gate, w_up, w_down):
    return (jax.nn.silu(x @ w_gate) * (x @ w_up)) @ w_down


def causal_depthwise_conv(x, w):
    K = w.shape[1]
    T = x.shape[1]
    xp = jnp.pad(x, ((0, 0), (K - 1, 0), (0, 0)))
    y = xp[:, 0:T] * w[:, 0]
    for j in range(1, K):
        y = y + xp[:, j:j + T] * w[:, j]
    return y


def chunk_gated_delta_rule(q, k, v, g, beta):
    B, T, H, Dk = q.shape
    Dv = v.shape[-1]
    N = T // CHUNK

    def to_chunks(t):
        t = jnp.moveaxis(t, 2, 1)
        return t.reshape((B, H, N, CHUNK) + t.shape[3:])

    q = to_chunks(q) * (Dk ** -0.5)
    k = to_chunks(k)
    v = to_chunks(v)
    beta = to_chunks(beta)
    g = jnp.cumsum(to_chunks(g), axis=-1)
    idx = jnp.arange(CHUNK)
    causal = idx[:, None] >= idx[None, :]
    strict = idx[:, None] > idx[None, :]
    decay = jnp.exp(jnp.where(causal, g[..., :, None] - g[..., None, :], -jnp.inf))
    k_beta = k * beta[..., None]
    lower = jnp.where(strict, jnp.einsum('bhncd,bhnsd->bhncs', k_beta, k) * decay, 0.0) + jnp.eye(CHUNK, dtype=jnp.float32)
    rhs = jnp.concatenate([v * beta[..., None], k_beta * jnp.exp(g)[..., None]], axis=-1)
    sol = lax.linalg.triangular_solve(lower, rhs, left_side=True, lower=True, unit_diagonal=True)
    u, w = sol[..., :Dv], sol[..., Dv:]
    attn_intra = jnp.einsum('bhncd,bhnsd->bhncs', q, k) * decay
    q_dec = q * jnp.exp(g)[..., None]
    k_dec = k * jnp.exp(g[..., -1:] - g)[..., None]
    g_last = jnp.exp(g[..., -1])

    def step(S, xs):
        q_d, k_d, u_c, w_c, a_c, gl = xs
        v_new = u_c - jnp.einsum('bhcd,bhde->bhce', w_c, S)
        o = jnp.einsum('bhcd,bhde->bhce', q_d, S) + jnp.einsum('bhcs,bhse->bhce', a_c, v_new)
        S = S * gl[..., None, None] + jnp.einsum('bhcd,bhce->bhde', k_d, v_new)
        return S, o

    xs = tuple(jnp.moveaxis(t, 2, 0) for t in (q_dec, k_dec, u, w, attn_intra, g_last))
    S0 = jnp.zeros((B, H, Dk, Dv), jnp.float32)
    _, o = lax.scan(step, S0, xs)
    o = jnp.moveaxis(o, 0, 2).reshape(B, H, T, Dv)
    return jnp.moveaxis(o, 1, 2)


def setup_inputs(seed: int = 0) -> dict:
    key = jax.random.key(seed)
    ks = jax.random.split(key, 24)
    f32 = jnp.float32

    def nrm(k, shape, fan_in):
        return jax.random.normal(k, shape, f32) * (fan_in ** -0.5)

    def gain(k, shape):
        return 1.0 + 0.01 * jax.random.normal(k, shape, f32)

    L = DEPTH
    dt = jnp.exp(jax.random.uniform(ks[10], (L, DN_HEADS), f32, np.log(1e-3), np.log(1e-1)))
    return {
        "x": jax.random.normal(ks[0], (BATCH, SEQ, D_MODEL), f32),
        "ffn1_norm": gain(ks[1], (L, D_MODEL)),
        "ffn1_w_gate": nrm(ks[2], (L, D_MODEL, D_FF), D_MODEL),
        "ffn1_w_up": nrm(ks[3], (L, D_MODEL, D_FF), D_MODEL),
        "ffn1_w_down": nrm(ks[4], (L, D_FF, D_MODEL), D_FF),
        "mix_norm": gain(ks[5], (L, D_MODEL)),
        "w_in": nrm(ks[6], (L, D_MODEL, N_IN), D_MODEL),
        "conv_mixer_w": nrm(ks[7], (L, CONV_WIDTH, CONV_K), CONV_K),
        "dn_conv_w": nrm(ks[8], (L, 3 * DN_WIDTH, DN_CONV_K), DN_CONV_K),
        "dn_a_log": jnp.log(jax.random.uniform(ks[9], (L, DN_HEADS), f32, 1.0, 16.0)),
        "dn_dt_bias": dt + jnp.log(-jnp.expm1(-dt)),
        "dn_out_norm": gain(ks[11], (L, DN_HEAD_DIM)),
        "w_conv_branch": nrm(ks[12], (L, CONV_WIDTH, D_MODEL), CONV_WIDTH),
        "w_dn_branch": nrm(ks[13], (L, DN_WIDTH, D_MODEL), DN_WIDTH),
        "w_out": nrm(ks[14], (L, D_MODEL, D_MODEL), D_MODEL),
        "ffn2_norm": gain(ks[15], (L, D_MODEL)),
        "ffn2_w_gate": nrm(ks[16], (L, D_MODEL, D_FF), D_MODEL),
        "ffn2_w_up": nrm(ks[17], (L, D_MODEL, D_FF), D_MODEL),
        "ffn2_w_down": nrm(ks[18], (L, D_FF, D_MODEL), D_FF),
        "final_norm": gain(ks[19], (D_MODEL,)),
    }


def reference(x, ffn1_norm, ffn1_w_gate, ffn1_w_up, ffn1_w_down, mix_norm, w_in, conv_mixer_w, dn_conv_w, dn_a_log, dn_dt_bias, dn_out_norm, w_conv_branch, w_dn_branch, w_out, ffn2_norm, ffn2_w_gate, ffn2_w_up, ffn2_w_down, final_norm):
    B, T, _ = x.shape
    f32 = jnp.float32
    h = x
    for l in range(DEPTH):
        h = h + 0.5 * swiglu(rms_norm(h, ffn1_norm[l]), ffn1_w_gate[l], ffn1_w_up[l], ffn1_w_down[l])

        u = rms_norm(h, mix_norm[l])
        p = u @ w_in[l]
        cB, cC, cx, dq, dk, dv, dz, da, db, gc, gd = jnp.split(p, SPLIT_POINTS, axis=-1)

        y_conv = cB * causal_depthwise_conv(cC * cx, conv_mixer_w[l])

        qkv = jax.nn.silu(causal_depthwise_conv(jnp.concatenate([dq, dk, dv], axis=-1), dn_conv_w[l])).astype(f32)
        q = l2_normalize(qkv[..., :DN_WIDTH].reshape(B, T, DN_HEADS, DN_HEAD_DIM))
        k = l2_normalize(qkv[..., DN_WIDTH:2 * DN_WIDTH].reshape(B, T, DN_HEADS, DN_HEAD_DIM))
        v = qkv[..., 2 * DN_WIDTH:].reshape(B, T, DN_HEADS, DN_HEAD_DIM)
        g = -jnp.exp(dn_a_log[l].astype(f32)) * jax.nn.softplus(da.astype(f32) + dn_dt_bias[l].astype(f32))
        beta = jax.nn.sigmoid(db.astype(f32))
        o = chunk_gated_delta_rule(q, k, v, g, beta)
        o = rms_norm(o, dn_out_norm[l]) * jax.nn.silu(dz.astype(f32).reshape(B, T, DN_HEADS, DN_HEAD_DIM))
        y_dn = o.reshape(B, T, DN_WIDTH).astype(x.dtype)

        merged = jax.nn.sigmoid(gc) * (y_conv @ w_conv_branch[l]) + jax.nn.sigmoid(gd) * (y_dn @ w_dn_branch[l])
        h = h + merged @ w_out[l]

        h = h + 0.5 * swiglu(rms_norm(h, ffn2_norm[l]), ffn2_w_gate[l], ffn2_w_up[l], ffn2_w_down[l])
    return rms_norm(h, final_norm)
```

```python
import functools

import jax
import jax.numpy as jnp
from jax import lax
from jax.experimental import pallas as pl
from jax.experimental.pallas import tpu as pltpu

F32 = jnp.float32
BF16 = jnp.bfloat16
EPS = 1e-6
L2_EPS = 1e-6
CHUNK = 64
LANES = 128
SUBLANES = 8
ROW_CHUNK = 32
V7X_VMEM_LIMIT_BYTES = 56 * 1024 * 1024
HIGHEST = lax.Precision.HIGHEST


def _params(*sem):
    return pltpu.CompilerParams(dimension_semantics=sem, vmem_limit_bytes=V7X_VMEM_LIMIT_BYTES)


def _tile(n, pref, mult=LANES):
    if n <= pref:
        return n
    t = (pref // mult) * mult
    while t > 0 and n % t:
        t -= mult
    assert t > 0, (n, pref, mult)
    return t


def _single(block, index_map):
    return pl.BlockSpec(block, index_map, pipeline_mode=pl.Buffered(1))


def _rms(h, gain):
    return h * lax.rsqrt(jnp.mean(h * h, axis=-1, keepdims=True) + EPS) * gain


def _for_rows(n_rows, fn):
    def body(r, carry):
        fn(pl.ds(pl.multiple_of(r * ROW_CHUNK, ROW_CHUNK), ROW_CHUNK))
        return carry
    lax.fori_loop(0, n_rows // ROW_CHUNK, body, 0)


def _dot(a, b, **kw):
    return jnp.dot(a, b, preferred_element_type=F32, **kw)


def _dot_nt(a, b, **kw):
    return lax.dot_general(a, b, (((1,), (1,)), ((), ())), preferred_element_type=F32, **kw)


def _dot_tn(a, b, **kw):
    return lax.dot_general(a, b, (((0,), (0,)), ((), ())), preferred_element_type=F32, **kw)


def _ffn_kernel(h_ref, g_ref, wg_ref, wu_ref, wd_ref, ng_ref, *refs, final):
    if final:
        o_ref, xn_ref = refs
    else:
        o_ref, u_ref, xn_ref = refs
    j = pl.program_id(1)
    tm = h_ref.shape[0]

    @pl.when(j == 0)
    def _():
        def init(rows):
            h = h_ref[rows, :]
            xn_ref[rows, :] = _rms(h, g_ref[...]).astype(BF16)
            o_ref[rows, :] = h
        _for_rows(tm, init)

    xn = xn_ref[...]
    gate = _dot(xn, wg_ref[...])
    up = _dot(xn, wu_ref[...])
    act = (gate * jax.nn.sigmoid(gate) * (0.5 * up)).astype(BF16)
    o_ref[...] += _dot(act, wd_ref[...])

    @pl.when(j == pl.num_programs(1) - 1)
    def _():
        def fin(rows):
            y = _rms(o_ref[rows, :], ng_ref[...])
            if final:
                o_ref[rows, :] = y
            else:
                u_ref[rows, :] = y.astype(BF16)
        _for_rows(tm, fin)


def _ffn(h, gain, wg, wu, wd, next_gain, *, final):
    M, D = h.shape
    F = wg.shape[1]
    tm = _tile(M, 512, ROW_CHUNK)
    tf = _tile(F, 256)
    row = lambda i, j: (i, 0)
    out_shape = [jax.ShapeDtypeStruct((M, D), F32)]
    out_specs = [_single((tm, D), row)]
    if not final:
        out_shape.append(jax.ShapeDtypeStruct((M, D), BF16))
        out_specs.append(_single((tm, D), row))
    res = pl.pallas_call(
        functools.partial(_ffn_kernel, final=final),
        grid=(M // tm, F // tf),
        in_specs=[
            _single((tm, D), row),
            pl.BlockSpec((1, D), lambda i, j: (0, 0)),
            pl.BlockSpec((D, tf), lambda i, j: (0, j)),
            pl.BlockSpec((D, tf), lambda i, j: (0, j)),
            pl.BlockSpec((tf, D), lambda i, j: (j, 0)),
            pl.BlockSpec((1, D), lambda i, j: (0, 0)),
        ],
        out_specs=out_specs,
        out_shape=out_shape,
        scratch_shapes=[pltpu.VMEM((tm, D), BF16)],
        compiler_params=_params("parallel", "arbitrary"),
        name="ffn_final" if final else "ffn",
    )(h, gain.reshape(1, D), wg, wu, wd, next_gain.reshape(1, D))
    return res[0] if final else res


def _matmul_kernel(a_ref, b_ref, o_ref):
    o_ref[...] = _dot(a_ref[...], b_ref[...]).astype(o_ref.dtype)


def _matmul(a, b, out_dtype):
    M, K = a.shape
    N = b.shape[1]
    tm = _tile(M, 1024, SUBLANES)
    tn = _tile(N, 512)
    return pl.pallas_call(
        _matmul_kernel,
        grid=(M // tm, N // tn),
        in_specs=[pl.BlockSpec((tm, K), lambda i, j: (i, 0)),
                  pl.BlockSpec((K, tn), lambda i, j: (0, j))],
        out_specs=pl.BlockSpec((tm, tn), lambda i, j: (i, j)),
        out_shape=jax.ShapeDtypeStruct((M, N), out_dtype),
        compiler_params=_params("parallel", "arbitrary"),
        name="in_proj",
    )(a, b)


def _gates_kernel(u_ref, w_ref, alog_ref, bias_ref, o_ref, *, n_heads):
    x = _dot_nt(w_ref[...], u_ref[...])
    s = x + bias_ref[...]
    softplus = jnp.maximum(s, 0.0) + jnp.log1p(jnp.exp(-jnp.abs(s)))
    g = -jnp.exp(alog_ref[...]) * softplus
    row = lax.broadcasted_iota(jnp.int32, x.shape, 0)
    o_ref[...] = jnp.where(row < n_heads, g, jax.nn.sigmoid(x))


def _gates(u, w_ab_t, a_log, dt_bias):
    M, D = u.shape
    H = a_log.shape[0]
    tm = _tile(M, 512)
    pad = jnp.zeros((H,), F32)
    col = lambda v: jnp.concatenate([v.astype(F32), pad]).reshape(2 * H, 1)
    return pl.pallas_call(
        functools.partial(_gates_kernel, n_heads=H),
        grid=(M // tm,),
        in_specs=[pl.BlockSpec((tm, D), lambda i: (i, 0)),
                  pl.BlockSpec((2 * H, D), lambda i: (0, 0)),
                  pl.BlockSpec((2 * H, 1), lambda i: (0, 0)),
                  pl.BlockSpec((2 * H, 1), lambda i: (0, 0))],
        out_specs=pl.BlockSpec((2 * H, tm), lambda i: (0, i)),
        out_shape=jax.ShapeDtypeStruct((2 * H, M), F32),
        compiler_params=_params("parallel"),
        name="gates",
    )(u, w_ab_t, col(a_log), col(dt_bias))


def _causal_conv(z_ref, w_ref, n_rows, K):
    base = SUBLANES - (K - 1)
    acc = z_ref[base:base + n_rows, :] * w_ref[0:1, :]
    for k in range(1, K):
        acc = acc + z_ref[base + k:base + k + n_rows, :] * w_ref[k:k + 1, :]
    return acc


def _conv_gate_kernel(cb_ref, cc_ref, cx_ref, hc_ref, hx_ref, w_ref, o_ref, z_ref, *, tiles_per_seq, K):
    first = (pl.program_id(0) % tiles_per_seq) == 0
    n_rows = cb_ref.shape[0]
    halo = hc_ref[...] * hx_ref[...]
    z_ref[0:SUBLANES, :] = halo * jnp.where(first, 0.0, 1.0)
    z_ref[SUBLANES:, :] = cc_ref[...] * cx_ref[...]
    o_ref[...] = (cb_ref[...] * _causal_conv(z_ref, w_ref, n_rows, K)).astype(o_ref.dtype)


def _conv_gate(p, w_t, *, width, seq_len):
    M = p.shape[0]
    K = w_t.shape[0]
    tt = _tile(seq_len, 512, SUBLANES)
    tc = _tile(width, 512)
    nc = width // tc
    hb = tt // SUBLANES
    halo = lambda off: pl.BlockSpec((SUBLANES, tc), lambda i, j: (jnp.maximum(i * hb - 1, 0), j + off))
    return pl.pallas_call(
        functools.partial(_conv_gate_kernel, tiles_per_seq=seq_len // tt, K=K),
        grid=(M // tt, nc),
        in_specs=[pl.BlockSpec((tt, tc), lambda i, j: (i, j)),
                  pl.BlockSpec((tt, tc), lambda i, j: (i, j + nc)),
                  pl.BlockSpec((tt, tc), lambda i, j: (i, j + 2 * nc)),
                  halo(nc), halo(2 * nc),
                  pl.BlockSpec((K, tc), lambda i, j: (0, j))],
        out_specs=pl.BlockSpec((tt, tc), lambda i, j: (i, j)),
        out_shape=jax.ShapeDtypeStruct((M, width), BF16),
        scratch_shapes=[pltpu.VMEM((tt + SUBLANES, tc), F32)],
        compiler_params=_params("parallel", "arbitrary"),
        name="conv_gate",
    )(p, p, p, p, p, w_t)


def _qkv_conv_kernel(x_ref, hx_ref, w_ref, o_ref, z_ref, *, tiles_per_seq, K, n_norm_tiles):
    first = (pl.program_id(0) % tiles_per_seq) == 0
    j = pl.program_id(1)
    n_rows, tc = x_ref.shape
    z_ref[0:SUBLANES, :] = hx_ref[...] * jnp.where(first, 0.0, 1.0)
    z_ref[SUBLANES:, :] = x_ref[...]
    acc = _causal_conv(z_ref, w_ref, n_rows, K)
    y = acc * jax.nn.sigmoid(acc)

    @pl.when(j < n_norm_tiles)
    def _():
        for hh in range(tc // LANES):
            yh = y[:, hh * LANES:(hh + 1) * LANES]
            ss = jnp.sum(yh * yh, axis=-1, keepdims=True)
            o_ref[:, hh * LANES:(hh + 1) * LANES] = yh * lax.rsqrt(ss + L2_EPS)

    @pl.when(j >= n_norm_tiles)
    def _():
        o_ref[...] = y


def _qkv_conv(p, w_t, *, col0, width, seq_len):
    M = p.shape[0]
    K = w_t.shape[0]
    tt = _tile(seq_len, 512, SUBLANES)
    tc = _tile(width, 512)
    nc = width // tc
    off = col0 // tc
    assert col0 % tc == 0
    hb = tt // SUBLANES
    return pl.pallas_call(
        functools.partial(_qkv_conv_kernel, tiles_per_seq=seq_len // tt, K=K, n_norm_tiles=2 * nc),
        grid=(M // tt, 3 * nc),
        in_specs=[pl.BlockSpec((tt, tc), lambda i, j: (i, j + off)),
                  pl.BlockSpec((SUBLANES, tc), lambda i, j: (jnp.maximum(i * hb - 1, 0), j + off)),
                  pl.BlockSpec((K, tc), lambda i, j: (0, j))],
        out_specs=pl.BlockSpec((tt, tc), lambda i, j: (i, j)),
        out_shape=jax.ShapeDtypeStruct((M, 3 * width), F32),
        scratch_shapes=[pltpu.VMEM((tt + SUBLANES, tc), F32)],
        compiler_params=_params("parallel", "arbitrary"),
        name="qkv_conv",
    )(p, p, w_t)


def _delta_kernel(q_ref, k_ref, v_ref, z_ref, g_ref, b_ref, gain_ref, o_ref,
                  gc_ref, u_ref, w_ref, a_ref, *, n_heads, n_chunks, group):
    C = CHUNK
    scale = LANES ** -0.5
    row = lax.broadcasted_iota(jnp.int32, (C, C), 0)
    col = lax.broadcasted_iota(jnp.int32, (C, C), 1)
    eye = row == col
    causal = row >= col
    strict = row > col
    eye_f = jnp.where(eye, 1.0, 0.0)

    prefix = jnp.where(row <= col, 1.0, 0.0)
    for hh in range(n_heads):
        gc_ref[hh] = _dot(g_ref[hh, 0], prefix, precision=HIGHEST)

    def to_col(r):
        return jnp.sum(jnp.where(eye, jnp.broadcast_to(r, (C, C)), 0.0), axis=1, keepdims=True)

    def chunk_vectors(hh, c):
        g_row = gc_ref[hh, pl.ds(c, 1), :]
        return g_row, to_col(g_row)

    def local(hh, c):
        rows = pl.ds(pl.multiple_of(c * C, C), C)
        lanes = slice(hh * LANES, (hh + 1) * LANES)
        q = q_ref[rows, lanes] * scale
        k = k_ref[rows, lanes]
        v = v_ref[rows, lanes]
        g_row, g_col = chunk_vectors(hh, c)
        b_col = to_col(b_ref[hh, 0, pl.ds(c, 1), :])
        decay = jnp.where(causal, jnp.exp(jnp.where(causal, g_col - g_row, 0.0)), 0.0)
        kb = k * b_col
        k16 = k.astype(BF16)
        a = jnp.where(strict, _dot_nt(kb.astype(BF16), k16) * decay, 0.0)
        inv = eye_f - a
        pw = a
        for _ in range(5):
            pw = _dot(pw, pw, precision=HIGHEST)
            inv = inv + _dot(inv, pw, precision=HIGHEST)
        rhs = jnp.concatenate([v * b_col, kb * jnp.exp(g_col)], axis=1)
        sol = _dot(inv, rhs, precision=HIGHEST)
        u_ref[hh, rows, :] = sol[:, :LANES]
        w_ref[hh, rows, :] = sol[:, LANES:]
        a_ref[hh, rows, :] = jnp.where(causal, _dot_nt(q.astype(BF16), k16) * decay, 0.0)

    def local_group(gi, carry):
        for hh in range(n_heads):
            for t in range(group):
                local(hh, gi * group + t)
        return carry

    lax.fori_loop(0, n_chunks // group, local_group, 0)

    def recur(c, states):
        rows = pl.ds(pl.multiple_of(c * C, C), C)
        out = []
        for hh in range(n_heads):
            S = states[hh]
            lanes = slice(hh * LANES, (hh + 1) * LANES)
            q = q_ref[rows, lanes] * scale
            k = k_ref[rows, lanes]
            g_row, g_col = chunk_vectors(hh, c)
            g_last = g_row[:, C - 1:C]
            q_dec = q * jnp.exp(g_col)
            k_dec = k * jnp.exp(g_last - g_col)
            S16 = S.astype(BF16)
            v_new = u_ref[hh, rows, :] - _dot(w_ref[hh, rows, :].astype(BF16), S16)
            v16 = v_new.astype(BF16)
            o = _dot(q_dec.astype(BF16), S16) + _dot(a_ref[hh, rows, :].astype(BF16), v16)
            out.append(S * jnp.exp(g_last) + _dot_tn(k_dec.astype(BF16), v16))
            z = z_ref[rows, lanes]
            o_ref[rows, lanes] = (_rms(o, gain_ref[...]) * (z * jax.nn.sigmoid(z))).astype(o_ref.dtype)
        return tuple(out)

    lax.fori_loop(0, n_chunks, recur, tuple(jnp.zeros((LANES, LANES), F32) for _ in range(n_heads)))


def _delta(qkv, p, gb, gain, *, z_col0, n_heads, seq_len, heads_per_step=1, group=4):
    M = qkv.shape[0]
    W = n_heads * LANES
    B = M // seq_len
    N = seq_len // CHUNK
    nh = heads_per_step
    hw = nh * LANES
    nb = W // hw
    assert z_col0 % hw == 0 and N % group == 0
    zoff = z_col0 // hw
    tok = lambda off: pl.BlockSpec((seq_len, hw), lambda b, h: (b, h + off))
    return pl.pallas_call(
        functools.partial(_delta_kernel, n_heads=nh, n_chunks=N, group=group),
        grid=(B, nb),
        in_specs=[tok(0), tok(nb), tok(2 * nb),
                  pl.BlockSpec((seq_len, hw), lambda b, h: (b, h + zoff)),
                  pl.BlockSpec((nh, 1, N, CHUNK), lambda b, h: (h, b, 0, 0)),
                  pl.BlockSpec((nh, 1, N, CHUNK), lambda b, h: (h + nb, b, 0, 0)),
                  pl.BlockSpec((1, LANES), lambda b, h: (0, 0))],
        out_specs=pl.BlockSpec((seq_len, hw), lambda b, h: (b, h)),
        out_shape=jax.ShapeDtypeStruct((M, W), BF16),
        scratch_shapes=[pltpu.VMEM((nh, N, CHUNK), F32),
                        pltpu.VMEM((nh, seq_len, LANES), F32),
                        pltpu.VMEM((nh, seq_len, LANES), F32),
                        pltpu.VMEM((nh, seq_len, CHUNK), F32)],
        compiler_params=_params("parallel", "arbitrary"),
        name="delta_rule",
    )(qkv, qkv, qkv, p, gb, gb, gain.reshape(1, LANES))


def _merge_kernel(u_ref, yc_ref, yd_ref, h_ref, wgc_ref, wgd_ref, wc_ref, wdn_ref, wo_ref, o_ref):
    @pl.when(pl.program_id(1) == 0)
    def _():
        def init(rows):
            o_ref[rows, :] = h_ref[rows, :]
        _for_rows(h_ref.shape[0], init)

    u = u_ref[...]
    gc = jax.nn.sigmoid(_dot(u, wgc_ref[...]))
    gd = jax.nn.sigmoid(_dot(u, wgd_ref[...]))
    merged = gc * _dot(yc_ref[...], wc_ref[...]) + gd * _dot(yd_ref[...], wdn_ref[...])
    o_ref[...] += _dot(merged.astype(BF16), wo_ref[...])


def _merge(u, y_conv, y_dn, h, wgc, wgd, wc, wdn, wo):
    M, D = h.shape
    Wc = y_conv.shape[1]
    Wd = y_dn.shape[1]
    tm = _tile(M, 512, ROW_CHUNK)
    tn = _tile(D, 256)
    row = lambda i, j: (i, 0)
    colw = lambda i, j: (0, j)
    return pl.pallas_call(
        _merge_kernel,
        grid=(M // tm, D // tn),
        in_specs=[_single((tm, D), row), _single((tm, Wc), row), _single((tm, Wd), row), _single((tm, D), row),
                  pl.BlockSpec((D, tn), colw), pl.BlockSpec((D, tn), colw),
                  pl.BlockSpec((Wc, tn), colw), pl.BlockSpec((Wd, tn), colw),
                  pl.BlockSpec((tn, D), lambda i, j: (j, 0))],
        out_specs=_single((tm, D), row),
        out_shape=jax.ShapeDtypeStruct((M, D), F32),
        compiler_params=_params("parallel", "arbitrary"),
        name="merge_out",
    )(u, y_conv, y_dn, h, wgc, wgd, wc, wdn, wo)


def kernel(x, ffn1_norm, ffn1_w_gate, ffn1_w_up, ffn1_w_down, mix_norm, w_in, conv_mixer_w, dn_conv_w, dn_a_log, dn_dt_bias, dn_out_norm, w_conv_branch, w_dn_branch, w_out, ffn2_norm, ffn2_w_gate, ffn2_w_up, ffn2_w_down, final_norm):
    B, T, D = x.shape
    M = B * T
    depth = ffn1_norm.shape[0]
    Wc = conv_mixer_w.shape[1]
    Wd = dn_conv_w.shape[1] // 3
    H = dn_a_log.shape[1]
    assert Wd == H * LANES and dn_out_norm.shape[1] == LANES and T % CHUNK == 0
    c_q = 3 * Wc
    c_z = c_q + 3 * Wd
    c_a = c_z + Wd
    c_gc = c_a + 2 * H
    c_gd = c_gc + D
    assert w_in.shape[2] == c_gd + D
    bf = lambda w: w.astype(BF16)

    h = x.reshape(M, D)
    for l in range(depth):
        last = l == depth - 1
        h, u = _ffn(h, ffn1_norm[l], bf(ffn1_w_gate[l]), bf(ffn1_w_up[l]), bf(ffn1_w_down[l]), mix_norm[l], final=False)

        wl = w_in[l]
        p = _matmul(u, bf(wl[:, :c_a]), F32)
        gb = _gates(u, bf(wl[:, c_a:c_gc].T), dn_a_log[l], dn_dt_bias[l])
        y_conv = _conv_gate(p, conv_mixer_w[l].T, width=Wc, seq_len=T)
        qkv = _qkv_conv(p, dn_conv_w[l].T, col0=c_q, width=Wd, seq_len=T)
        y_dn = _delta(qkv, p, gb.reshape(2 * H, B, T // CHUNK, CHUNK), dn_out_norm[l],
                      z_col0=c_z, n_heads=H, seq_len=T)
        h = _merge(u, y_conv, y_dn, h, bf(wl[:, c_gc:c_gd]), bf(wl[:, c_gd:]),
                   bf(w_conv_branch[l]), bf(w_dn_branch[l]), bf(w_out[l]))

        if last:
            h = _ffn(h, ffn2_norm[l], bf(ffn2_w_gate[l]), bf(ffn2_w_up[l]), bf(ffn2_w_down[l]), final_norm, final=True)
        else:
            h, _ = _ffn(h, ffn2_norm[l], bf(ffn2_w_gate[l]), bf(ffn2_w_up[l]), bf(ffn2_w_down[l]), final_norm, final=False)
    return h.reshape(B, T, D)
```

```python
import functools

import jax
import jax.numpy as jnp
from jax import lax
from jax.experimental import pallas as pl
from jax.experimental.pallas import tpu as pltpu

F32 = jnp.float32
BF16 = jnp.bfloat16
EPS = 1e-6
L2_EPS = 1e-6
CHUNK = 64
LANES = 128
SUBLANES = 8
ROW_CHUNK = 32
V7X_VMEM_LIMIT_BYTES = 56 * 1024 * 1024
HIGHEST = lax.Precision.HIGHEST


def _params(*sem):
    return pltpu.CompilerParams(dimension_semantics=sem, vmem_limit_bytes=V7X_VMEM_LIMIT_BYTES)


def _tile(n, pref, mult=LANES):
    if n <= pref:
        return n
    t = (pref // mult) * mult
    while t > 0 and n % t:
        t -= mult
    assert t > 0, (n, pref, mult)
    return t


def _single(block, index_map):
    return pl.BlockSpec(block, index_map, pipeline_mode=pl.Buffered(1))


def _rms(h, gain):
    return h * lax.rsqrt(jnp.mean(h * h, axis=-1, keepdims=True) + EPS) * gain


def _for_rows(n_rows, fn):
    def body(r, carry):
        fn(pl.ds(pl.multiple_of(r * ROW_CHUNK, ROW_CHUNK), ROW_CHUNK))
        return carry
    lax.fori_loop(0, n_rows // ROW_CHUNK, body, 0)


def _dot(a, b, **kw):
    return jnp.dot(a, b, preferred_element_type=F32, **kw)


def _dot_nt(a, b, **kw):
    return lax.dot_general(a, b, (((1,), (1,)), ((), ())), preferred_element_type=F32, **kw)


def _dot_tn(a, b, **kw):
    return lax.dot_general(a, b, (((0,), (0,)), ((), ())), preferred_element_type=F32, **kw)


def _ffn_kernel(h_ref, g_ref, wg_ref, wu_ref, wd_ref, ng_ref, *refs, final):
    if final:
        o_ref, xn_ref = refs
    else:
        o_ref, u_ref, xn_ref = refs
    j = pl.program_id(1)
    tm = h_ref.shape[0]

    @pl.when(j == 0)
    def _():
        def init(rows):
            h = h_ref[rows, :]
            xn_ref[rows, :] = _rms(h, g_ref[...]).astype(BF16)
            o_ref[rows, :] = h
        _for_rows(tm, init)

    xn = xn_ref[...]
    gate = _dot(xn, wg_ref[...])
    up = _dot(xn, wu_ref[...])
    act = (gate * jax.nn.sigmoid(gate) * (0.5 * up)).astype(BF16)
    o_ref[...] += _dot(act, wd_ref[...])

    @pl.when(j == pl.num_programs(1) - 1)
    def _():
        def fin(rows):
            y = _rms(o_ref[rows, :], ng_ref[...])
            if final:
                o_ref[rows, :] = y
            else:
                u_ref[rows, :] = y.astype(BF16)
        _for_rows(tm, fin)


def _ffn(h, gain, wg, wu, wd, next_gain, *, final):
    M, D = h.shape
    F = wg.shape[1]
    tm = _tile(M, 512, ROW_CHUNK)
    tf = _tile(F, 256)
    row = lambda i, j: (i, 0)
    out_shape = [jax.ShapeDtypeStruct((M, D), F32)]
    out_specs = [_single((tm, D), row)]
    if not final:
        out_shape.append(jax.ShapeDtypeStruct((M, D), BF16))
        out_specs.append(_single((tm, D), row))
    res = pl.pallas_call(
        functools.partial(_ffn_kernel, final=final),
        grid=(M // tm, F // tf),
        in_specs=[
            _single((tm, D), row),
            pl.BlockSpec((1, D), lambda i, j: (0, 0)),
            pl.BlockSpec((D, tf), lambda i, j: (0, j)),
            pl.BlockSpec((D, tf), lambda i, j: (0, j)),
            pl.BlockSpec((tf, D), lambda i, j: (j, 0)),
            pl.BlockSpec((1, D), lambda i, j: (0, 0)),
        ],
        out_specs=out_specs,
        out_shape=out_shape,
        scratch_shapes=[pltpu.VMEM((tm, D), BF16)],
        compiler_params=_params("parallel", "arbitrary"),
        name="ffn_final" if final else "ffn",
    )(h, gain.reshape(1, D), wg, wu, wd, next_gain.reshape(1, D))
    return res[0] if final else res


def _matmul_kernel(a_ref, b_ref, o_ref):
    o_ref[...] = _dot(a_ref[...], b_ref[...]).astype(o_ref.dtype)


def _matmul(a, b, out_dtype):
    M, K = a.shape
    N = b.shape[1]
    tm = _tile(M, 1024, SUBLANES)
    tn = _tile(N, 512)
    return pl.pallas_call(
        _matmul_kernel,
        grid=(M // tm, N // tn),
        in_specs=[pl.BlockSpec((tm, K), lambda i, j: (i, 0)),
                  pl.BlockSpec((K, tn), lambda i, j: (0, j))],
        out_specs=pl.BlockSpec((tm, tn), lambda i, j: (i, j)),
        out_shape=jax.ShapeDtypeStruct((M, N), out_dtype),
        compiler_params=_params("parallel", "arbitrary"),
        name="in_proj",
    )(a, b)


def _gates_kernel(u_ref, w_ref, alog_ref, bias_ref, o_ref, *, n_heads):
    x = _dot_nt(w_ref[...], u_ref[...])
    s = x + bias_ref[...]
    softplus = jnp.maximum(s, 0.0) + jnp.log1p(jnp.exp(-jnp.abs(s)))
    g = -jnp.exp(alog_ref[...]) * softplus
    row = lax.broadcasted_iota(jnp.int32, x.shape, 0)
    o_ref[...] = jnp.where(row < n_heads, g, jax.nn.sigmoid(x))


def _gates(u, w_ab_t, a_log, dt_bias):
    M, D = u.shape
    H = a_log.shape[0]
    tm = _tile(M, 512)
    pad = jnp.zeros((H,), F32)
    col = lambda v: jnp.concatenate([v.astype(F32), pad]).reshape(2 * H, 1)
    return pl.pallas_call(
        functools.partial(_gates_kernel, n_heads=H),
        grid=(M // tm,),
        in_specs=[pl.BlockSpec((tm, D), lambda i: (i, 0)),
                  pl.BlockSpec((2 * H, D), lambda i: (0, 0)),
                  pl.BlockSpec((2 * H, 1), lambda i: (0, 0)),
                  pl.BlockSpec((2 * H, 1), lambda i: (0, 0))],
        out_specs=pl.BlockSpec((2 * H, tm), lambda i: (0, i)),
        out_shape=jax.ShapeDtypeStruct((2 * H, M), F32),
        compiler_params=_params("parallel"),
        name="gates",
    )(u, w_ab_t, col(a_log), col(dt_bias))


def _causal_conv(z_ref, w_ref, n_rows, K):
    base = SUBLANES - (K - 1)
    acc = z_ref[base:base + n_rows, :] * w_ref[0:1, :]
    for k in range(1, K):
        acc = acc + z_ref[base + k:base + k + n_rows, :] * w_ref[k:k + 1, :]
    return acc


def _conv_gate_kernel(cb_ref, cc_ref, cx_ref, hc_ref, hx_ref, w_ref, o_ref, z_ref, *, tiles_per_seq, K):
    first = (pl.program_id(0) % tiles_per_seq) == 0
    n_rows = cb_ref.shape[0]
    halo = hc_ref[...] * hx_ref[...]
    z_ref[0:SUBLANES, :] = halo * jnp.where(first, 0.0, 1.0)
    z_ref[SUBLANES:, :] = cc_ref[...] * cx_ref[...]
    o_ref[...] = (cb_ref[...] * _causal_conv(z_ref, w_ref, n_rows, K)).astype(o_ref.dtype)


def _conv_gate(p, w_t, *, width, seq_len):
    M = p.shape[0]
    K = w_t.shape[0]
    tt = _tile(seq_len, 512, SUBLANES)
    tc = _tile(width, 512)
    nc = width // tc
    hb = tt // SUBLANES
    halo = lambda off: pl.BlockSpec((SUBLANES, tc), lambda i, j: (jnp.maximum(i * hb - 1, 0), j + off))
    return pl.pallas_call(
        functools.partial(_conv_gate_kernel, tiles_per_seq=seq_len // tt, K=K),
        grid=(M // tt, nc),
        in_specs=[pl.BlockSpec((tt, tc), lambda i, j: (i, j)),
                  pl.BlockSpec((tt, tc), lambda i, j: (i, j + nc)),
                  pl.BlockSpec((tt, tc), lambda i, j: (i, j + 2 * nc)),
                  halo(nc), halo(2 * nc),
                  pl.BlockSpec((K, tc), lambda i, j: (0, j))],
        out_specs=pl.BlockSpec((tt, tc), lambda i, j: (i, j)),
        out_shape=jax.ShapeDtypeStruct((M, width), BF16),
        scratch_shapes=[pltpu.VMEM((tt + SUBLANES, tc), F32)],
        compiler_params=_params("parallel", "arbitrary"),
        name="conv_gate",
    )(p, p, p, p, p, w_t)


def _qkv_conv_kernel(x_ref, hx_ref, w_ref, o_ref, z_ref, *, tiles_per_seq, K, n_norm_tiles):
    first = (pl.program_id(0) % tiles_per_seq) == 0
    j = pl.program_id(1)
    n_rows, tc = x_ref.shape
    z_ref[0:SUBLANES, :] = hx_ref[...] * jnp.where(first, 0.0, 1.0)
    z_ref[SUBLANES:, :] = x_ref[...]
    acc = _causal_conv(z_ref, w_ref, n_rows, K)
    y = acc * jax.nn.sigmoid(acc)

    @pl.when(j < n_norm_tiles)
    def _():
        for hh in range(tc // LANES):
            yh = y[:, hh * LANES:(hh + 1) * LANES]
            ss = jnp.sum(yh * yh, axis=-1, keepdims=True)
            o_ref[:, hh * LANES:(hh + 1) * LANES] = (yh * lax.rsqrt(ss + L2_EPS)).astype(o_ref.dtype)

    @pl.when(j >= n_norm_tiles)
    def _():
        o_ref[...] = y.astype(o_ref.dtype)


def _qkv_conv(p, w_t, *, col0, width, seq_len):
    M = p.shape[0]
    K = w_t.shape[0]
    tt = _tile(seq_len, 512, SUBLANES)
    tc = _tile(width, 512)
    nc = width // tc
    off = col0 // tc
    assert col0 % tc == 0
    hb = tt // SUBLANES
    return pl.pallas_call(
        functools.partial(_qkv_conv_kernel, tiles_per_seq=seq_len // tt, K=K, n_norm_tiles=2 * nc),
        grid=(M // tt, 3 * nc),
        in_specs=[pl.BlockSpec((tt, tc), lambda i, j: (i, j + off)),
                  pl.BlockSpec((SUBLANES, tc), lambda i, j: (jnp.maximum(i * hb - 1, 0), j + off)),
                  pl.BlockSpec((K, tc), lambda i, j: (0, j))],
        out_specs=pl.BlockSpec((tt, tc), lambda i, j: (i, j)),
        out_shape=jax.ShapeDtypeStruct((M, 3 * width), BF16),
        scratch_shapes=[pltpu.VMEM((tt + SUBLANES, tc), F32)],
        compiler_params=_params("parallel", "arbitrary"),
        name="qkv_conv",
    )(p, p, w_t)


A_PITCH = CHUNK + SUBLANES


def _delta_kernel(q_ref, k_ref, v_ref, z_ref, g_ref, b_ref, gain_ref, o_ref,
                  gc_ref, s_ref, am_ref, at_ref, tm_ref, u_ref, wq_ref, kt_ref, a_ref, *, n_heads, n_chunks):
    C = CHUNK
    nc = n_heads * n_chunks
    scale = LANES ** -0.5
    row = lax.broadcasted_iota(jnp.int32, (C, C), 0)
    col = lax.broadcasted_iota(jnp.int32, (C, C), 1)
    eye = row == col
    causal = row >= col
    strict = row > col
    zeros_cc = jnp.zeros((C, C), F32)

    @pl.when(pl.program_id(2) == 0)
    def _():
        s_ref[...] = jnp.zeros_like(s_ref)

    prefix = jnp.where(row <= col, 1.0, 0.0)
    for hh in range(n_heads):
        gc_ref[hh] = _dot(g_ref[hh, 0], prefix, precision=HIGHEST)

    def to_col(r):
        return jnp.sum(jnp.where(eye, jnp.broadcast_to(r, (C, C)), 0.0), axis=1, keepdims=True)

    def chunk_slices(hh, c):
        return pl.ds(pl.multiple_of(c * C, C), C), slice(hh * LANES, (hh + 1) * LANES)

    def am_rows(hh, c):
        return pl.ds(pl.multiple_of((hh * n_chunks + c) * A_PITCH, SUBLANES), C)

    def intra(c, carry):
        for hh in range(n_heads):
            rows, lanes = chunk_slices(hh, c)
            k16 = k_ref[rows, lanes]
            k = k16.astype(F32)
            q = q_ref[rows, lanes].astype(F32) * scale
            g_row = gc_ref[hh, pl.ds(c, 1), :]
            g_col = to_col(g_row)
            b_col = to_col(b_ref[hh, 0, pl.ds(c, 1), :])
            decay = jnp.where(causal, jnp.exp(jnp.where(causal, g_col - g_row, 0.0)), 0.0)
            prod = _dot_nt(jnp.concatenate([k * b_col, q], axis=0).astype(BF16), k16)
            a = jnp.where(strict, prod[:C] * decay, 0.0)
            am_ref[am_rows(hh, c), :] = jnp.concatenate([a, zeros_cc], axis=1)
            a_ref[hh, rows, :] = (prod[C:] * decay).astype(BF16)
        return carry

    lax.fori_loop(0, n_chunks, intra, 0)

    for i in range(C):
        at_ref[i] = am_ref[pl.ds(i, nc, stride=A_PITCH), :].T[:C, :]

    zero_blk = jnp.zeros((SUBLANES, nc), F32)
    for i in range(C):
        nb = -(-i // SUBLANES)
        acc = [-at_ref[i, b * SUBLANES:(b + 1) * SUBLANES, :] for b in range(nb)]
        for j in range(1, i):
            a_ij = jnp.broadcast_to(at_ref[i, j:j + 1, :], (SUBLANES, nc))
            for b in range(-(-j // SUBLANES)):
                acc[b] = acc[b] - a_ij * tm_ref[j, b * SUBLANES:(b + 1) * SUBLANES, :]
        for b in range(C // SUBLANES):
            tm_ref[i, b * SUBLANES:(b + 1) * SUBLANES, :] = acc[b] if b < nb else zero_blk

    zeros_pad = jnp.zeros((LANES - C, nc), F32)
    for i in range(C):
        am_ref[pl.ds(i, nc, stride=A_PITCH), :] = jnp.concatenate([tm_ref[i], zeros_pad], axis=0).T

    def solve(c, carry):
        for hh in range(n_heads):
            rows, lanes = chunk_slices(hh, c)
            k = k_ref[rows, lanes].astype(F32)
            v = v_ref[rows, lanes].astype(F32)
            q = q_ref[rows, lanes].astype(F32) * scale
            g_row = gc_ref[hh, pl.ds(c, 1), :]
            g_col = to_col(g_row)
            b_col = to_col(b_ref[hh, 0, pl.ds(c, 1), :])
            e_col = jnp.exp(g_col)
            rhs = jnp.concatenate([v * b_col, k * (b_col * e_col)], axis=1)
            tm = am_ref[am_rows(hh, c), :][:, :C]
            sol = rhs + _dot(tm.astype(BF16), rhs.astype(BF16))
            u_ref[hh, rows, :] = sol[:, :LANES]
            wq_ref[hh, pl.ds(pl.multiple_of(c * 2 * C, 2 * C), C), :] = sol[:, LANES:].astype(BF16)
            wq_ref[hh, pl.ds(pl.multiple_of(c * 2 * C + C, C), C), :] = (q * e_col).astype(BF16)
            k_dec = k * jnp.exp(g_row[:, C - 1:C] - g_col)
            kt_ref[hh, pl.ds(pl.multiple_of(c * LANES, LANES), LANES), :] = k_dec.T.astype(BF16)
        return carry

    lax.fori_loop(0, n_chunks, solve, 0)

    def recur(c, carry):
        heads = range(n_heads)
        rows = pl.ds(pl.multiple_of(c * C, C), C)
        S = [s_ref[hh] for hh in heads]
        r = [_dot(wq_ref[hh, pl.ds(pl.multiple_of(c * 2 * C, 2 * C), 2 * C), :], S[hh].astype(BF16))
             for hh in heads]
        v16 = [(u_ref[hh, rows, :] - r[hh][:C]).astype(BF16) for hh in heads]
        for hh in heads:
            g_last = gc_ref[hh, pl.ds(c, 1), :][:, C - 1:C]
            kt = kt_ref[hh, pl.ds(pl.multiple_of(c * LANES, LANES), LANES), :]
            s_ref[hh] = S[hh] * jnp.exp(g_last) + _dot(kt, v16[hh])
        for hh in heads:
            lanes = slice(hh * LANES, (hh + 1) * LANES)
            o = r[hh][C:] + _dot(a_ref[hh, rows, :], v16[hh])
            z = z_ref[rows, lanes]
            o_ref[rows, lanes] = (_rms(o, gain_ref[...]) * (z * jax.nn.sigmoid(z))).astype(o_ref.dtype)
        return carry

    lax.fori_loop(0, n_chunks, recur, 0)


def _delta(qkv, p, gb, gain, *, z_col0, n_heads, seq_len):
    M = qkv.shape[0]
    W = n_heads * LANES
    B = M // seq_len
    nh = min(n_heads, 8)
    ncb = LANES // nh
    tb = ncb * CHUNK
    hw = nh * LANES
    nb = W // hw
    nt = seq_len // tb
    assert LANES % nh == 0 and n_heads % nh == 0 and seq_len % tb == 0 and z_col0 % hw == 0 and ncb % SUBLANES == 0
    zoff = z_col0 // hw
    tok = lambda off: pl.BlockSpec((tb, hw), lambda b, h, t: (b * nt + t, h + off))
    return pl.pallas_call(
        functools.partial(_delta_kernel, n_heads=nh, n_chunks=ncb),
        grid=(B, nb, nt),
        in_specs=[tok(0), tok(nb), tok(2 * nb), tok(zoff),
                  pl.BlockSpec((nh, 1, ncb, CHUNK), lambda b, h, t: (h, b, t, 0)),
                  pl.BlockSpec((nh, 1, ncb, CHUNK), lambda b, h, t: (h + nb, b, t, 0)),
                  pl.BlockSpec((1, LANES), lambda b, h, t: (0, 0))],
        out_specs=tok(0),
        out_shape=jax.ShapeDtypeStruct((M, W), BF16),
        scratch_shapes=[pltpu.VMEM((nh, ncb, CHUNK), F32),
                        pltpu.VMEM((nh, LANES, LANES), F32),
                        pltpu.VMEM((LANES * A_PITCH, LANES), F32),
                        pltpu.VMEM((CHUNK, CHUNK, LANES), F32),
                        pltpu.VMEM((CHUNK, CHUNK, LANES), F32),
                        pltpu.VMEM((nh, tb, LANES), F32),
                        pltpu.VMEM((nh, 2 * tb, LANES), BF16),
                        pltpu.VMEM((nh, ncb * LANES, CHUNK), BF16),
                        pltpu.VMEM((nh, tb, CHUNK), BF16)],
        compiler_params=_params("parallel", "parallel", "arbitrary"),
        name="delta_rule",
    )(qkv, qkv, qkv, p, gb, gb, gain.reshape(1, LANES))


def _merge_kernel(u_ref, yc_ref, yd_ref, h_ref, wgc_ref, wgd_ref, wc_ref, wdn_ref, wo_ref, o_ref):
    @pl.when(pl.program_id(1) == 0)
    def _():
        def init(rows):
            o_ref[rows, :] = h_ref[rows, :]
        _for_rows(h_ref.shape[0], init)

    u = u_ref[...]
    gc = jax.nn.sigmoid(_dot(u, wgc_ref[...]))
    gd = jax.nn.sigmoid(_dot(u, wgd_ref[...]))
    merged = gc * _dot(yc_ref[...], wc_ref[...]) + gd * _dot(yd_ref[...], wdn_ref[...])
    o_ref[...] += _dot(merged.astype(BF16), wo_ref[...])


def _merge(u, y_conv, y_dn, h, wgc, wgd, wc, wdn, wo):
    M, D = h.shape
    Wc = y_conv.shape[1]
    Wd = y_dn.shape[1]
    tm = _tile(M, 512, ROW_CHUNK)
    tn = _tile(D, 256)
    row = lambda i, j: (i, 0)
    colw = lambda i, j: (0, j)
    return pl.pallas_call(
        _merge_kernel,
        grid=(M // tm, D // tn),
        in_specs=[_single((tm, D), row), _single((tm, Wc), row), _single((tm, Wd), row), _single((tm, D), row),
                  pl.BlockSpec((D, tn), colw), pl.BlockSpec((D, tn), colw),
                  pl.BlockSpec((Wc, tn), colw), pl.BlockSpec((Wd, tn), colw),
                  pl.BlockSpec((tn, D), lambda i, j: (j, 0))],
        out_specs=_single((tm, D), row),
        out_shape=jax.ShapeDtypeStruct((M, D), F32),
        compiler_params=_params("parallel", "arbitrary"),
        name="merge_out",
    )(u, y_conv, y_dn, h, wgc, wgd, wc, wdn, wo)


def kernel(x, ffn1_norm, ffn1_w_gate, ffn1_w_up, ffn1_w_down, mix_norm, w_in, conv_mixer_w, dn_conv_w, dn_a_log, dn_dt_bias, dn_out_norm, w_conv_branch, w_dn_branch, w_out, ffn2_norm, ffn2_w_gate, ffn2_w_up, ffn2_w_down, final_norm):
    B, T, D = x.shape
    M = B * T
    depth = ffn1_norm.shape[0]
    Wc = conv_mixer_w.shape[1]
    Wd = dn_conv_w.shape[1] // 3
    H = dn_a_log.shape[1]
    assert Wd == H * LANES and dn_out_norm.shape[1] == LANES and T % CHUNK == 0
    c_q = 3 * Wc
    c_z = c_q + 3 * Wd
    c_a = c_z + Wd
    c_gc = c_a + 2 * H
    c_gd = c_gc + D
    assert w_in.shape[2] == c_gd + D
    bf = lambda w: w.astype(BF16)

    h = x.reshape(M, D)
    for l in range(depth):
        last = l == depth - 1
        h, u = _ffn(h, ffn1_norm[l], bf(ffn1_w_gate[l]), bf(ffn1_w_up[l]), bf(ffn1_w_down[l]), mix_norm[l], final=False)

        wl = w_in[l]
        p = _matmul(u, bf(wl[:, :c_a]), F32)
        gb = _gates(u, bf(wl[:, c_a:c_gc].T), dn_a_log[l], dn_dt_bias[l])
        y_conv = _conv_gate(p, conv_mixer_w[l].T, width=Wc, seq_len=T)
        qkv = _qkv_conv(p, dn_conv_w[l].T, col0=c_q, width=Wd, seq_len=T)
        y_dn = _delta(qkv, p, gb.reshape(2 * H, B, T // CHUNK, CHUNK), dn_out_norm[l],
                      z_col0=c_z, n_heads=H, seq_len=T)
        h = _merge(u, y_conv, y_dn, h, bf(wl[:, c_gc:c_gd]), bf(wl[:, c_gd:]),
                   bf(w_conv_branch[l]), bf(w_dn_branch[l]), bf(w_out[l]))

        if last:
            h = _ffn(h, ffn2_norm[l], bf(ffn2_w_gate[l]), bf(ffn2_w_up[l]), bf(ffn2_w_down[l]), final_norm, final=True)
        else:
            h, _ = _ffn(h, ffn2_norm[l], bf(ffn2_w_gate[l]), bf(ffn2_w_up[l]), bf(ffn2_w_down[l]), final_norm, final=False)
    return h.reshape(B, T, D)
```

```python
import functools

import jax
import jax.numpy as jnp
from jax import lax
from jax.experimental import pallas as pl
from jax.experimental.pallas import tpu as pltpu

F32 = jnp.float32
BF16 = jnp.bfloat16
EPS = 1e-6
L2_EPS = 1e-6
CHUNK = 64
LANES = 128
SUBLANES = 8
ROW_CHUNK = 32
V7X_VMEM_LIMIT_BYTES = 56 * 1024 * 1024
HIGHEST = lax.Precision.HIGHEST


def _params(*sem):
    return pltpu.CompilerParams(dimension_semantics=sem, vmem_limit_bytes=V7X_VMEM_LIMIT_BYTES)


def _tile(n, pref, mult=LANES):
    if n <= pref:
        return n
    t = (pref // mult) * mult
    while t > 0 and n % t:
        t -= mult
    assert t > 0, (n, pref, mult)
    return t


def _single(block, index_map):
    return pl.BlockSpec(block, index_map, pipeline_mode=pl.Buffered(1))


def _rms(h, gain):
    return h * lax.rsqrt(jnp.mean(h * h, axis=-1, keepdims=True) + EPS) * gain


def _for_rows(n_rows, fn):
    def body(r, carry):
        fn(pl.ds(pl.multiple_of(r * ROW_CHUNK, ROW_CHUNK), ROW_CHUNK))
        return carry
    lax.fori_loop(0, n_rows // ROW_CHUNK, body, 0)


def _dot(a, b, **kw):
    return jnp.dot(a, b, preferred_element_type=F32, **kw)


def _dot_nt(a, b, **kw):
    return lax.dot_general(a, b, (((1,), (1,)), ((), ())), preferred_element_type=F32, **kw)


def _dot_tn(a, b, **kw):
    return lax.dot_general(a, b, (((0,), (0,)), ((), ())), preferred_element_type=F32, **kw)


def _spread(shape, grid):
    R, Cn = shape
    gi, gj = grid
    best = None
    for shift in range(12):
        bc = -(-Cn // (LANES << shift)) * LANES
        ncol = -(-Cn // bc)
        if ncol != 1 << shift:
            continue
        nr_max = (gi * gj) // ncol
        if nr_max == 0:
            break
        br = -(-R // (nr_max * 2 * SUBLANES)) * 2 * SUBLANES
        nr = -(-R // br)
        key = (R % br == 0 and Cn % bc == 0 and 2 * nr * ncol >= gi * gj, nr * ncol)
        if best is None or key > best[0]:
            best = (key, br, bc, nr, shift)
    _, br, bc, nr, shift = best

    def index_map(i, j):
        s = jnp.minimum(i * gj + j, (nr << shift) - 1)
        return s >> shift, s & ((1 << shift) - 1)
    return (br, bc), index_map


def _ffn_kernel(h_ref, g_ref, wg_ref, wu_ref, wd_ref, ng_ref, *refs, final, n_casts):
    cast_in, refs = refs[:n_casts], refs[n_casts:]
    if final:
        o_ref, xn_ref = refs
    else:
        o_ref, u_ref, *cast_out, xn_ref = refs
        for src, dst in zip(cast_in, cast_out):
            dst[...] = src[...].astype(BF16)
    j = pl.program_id(1)
    tm = h_ref.shape[0]

    @pl.when(j == 0)
    def _():
        def init(rows):
            h = h_ref[rows, :]
            xn_ref[rows, :] = _rms(h, g_ref[...]).astype(BF16)
            o_ref[rows, :] = h
        _for_rows(tm, init)

    xn = xn_ref[...]
    gate = _dot(xn, wg_ref[...])
    up = _dot(xn, wu_ref[...])
    act = (gate * jax.nn.sigmoid(gate) * (0.5 * up)).astype(BF16)
    o_ref[...] += _dot(act, wd_ref[...])

    @pl.when(j == pl.num_programs(1) - 1)
    def _():
        def fin(rows):
            y = _rms(o_ref[rows, :], ng_ref[...])
            if final:
                o_ref[rows, :] = y
            else:
                u_ref[rows, :] = y.astype(BF16)
        _for_rows(tm, fin)


def _ffn(h, gain, wg, wu, wd, next_gain, *, final, casts=()):
    M, D = h.shape
    F = wg.shape[1]
    tm = _tile(M, 512, ROW_CHUNK)
    tf = _tile(F, 256)
    grid = (M // tm, F // tf)
    row = lambda i, j: (i, 0)
    out_shape = [jax.ShapeDtypeStruct((M, D), F32)]
    out_specs = [_single((tm, D), row)]
    cast_specs = []
    if not final:
        out_shape.append(jax.ShapeDtypeStruct((M, D), BF16))
        out_specs.append(_single((tm, D), row))
        for arr in casts:
            cast_specs.append(pl.BlockSpec(*_spread(arr.shape, grid)))
            out_shape.append(jax.ShapeDtypeStruct(arr.shape, BF16))
        out_specs += cast_specs
    res = pl.pallas_call(
        functools.partial(_ffn_kernel, final=final, n_casts=len(cast_specs)),
        grid=grid,
        in_specs=[
            pl.BlockSpec((tm, D), row),
            pl.BlockSpec((1, D), lambda i, j: (0, 0)),
            pl.BlockSpec((D, tf), lambda i, j: (0, j)),
            pl.BlockSpec((D, tf), lambda i, j: (0, j)),
            pl.BlockSpec((tf, D), lambda i, j: (j, 0)),
            pl.BlockSpec((1, D), lambda i, j: (0, 0)),
        ] + cast_specs,
        out_specs=out_specs,
        out_shape=out_shape,
        scratch_shapes=[pltpu.VMEM((tm, D), BF16)],
        compiler_params=_params("parallel", "arbitrary"),
        name="ffn_final" if final else "ffn",
    )(h, gain.reshape(1, D), wg, wu, wd, next_gain.reshape(1, D), *casts)
    return res[0] if final else res


def _matmul_kernel(a_ref, b_ref, o_ref):
    o_ref[...] = _dot(a_ref[...], b_ref[...]).astype(o_ref.dtype)


def _matmul(a, b, n_cols, out_dtype):
    M, K = a.shape
    N = n_cols
    tm = _tile(M, 1024, SUBLANES)
    tn = _tile(N, 512)
    return pl.pallas_call(
        _matmul_kernel,
        grid=(M // tm, N // tn),
        in_specs=[pl.BlockSpec((tm, K), lambda i, j: (i, 0)),
                  pl.BlockSpec((K, tn), lambda i, j: (0, j))],
        out_specs=pl.BlockSpec((tm, tn), lambda i, j: (i, j)),
        out_shape=jax.ShapeDtypeStruct((M, N), out_dtype),
        compiler_params=_params("parallel", "arbitrary"),
        name="in_proj",
    )(a, b)


def _gates_kernel(u_ref, w_ref, alog_ref, bias_ref, o_ref, *, n_heads):
    x = _dot_nt(w_ref[...], u_ref[...])
    s = x + bias_ref[...]
    softplus = jnp.maximum(s, 0.0) + jnp.log1p(jnp.exp(-jnp.abs(s)))
    g = -jnp.exp(alog_ref[...]) * softplus
    row = lax.broadcasted_iota(jnp.int32, x.shape, 0)
    o_ref[...] = jnp.where(row < n_heads, g, jax.nn.sigmoid(x))


def _gates(u, w_ab_t, a_log, dt_bias):
    M, D = u.shape
    H = a_log.shape[0]
    tm = _tile(M, 512)
    pad = jnp.zeros((H,), F32)
    col = lambda v: jnp.concatenate([v.astype(F32), pad]).reshape(2 * H, 1)
    return pl.pallas_call(
        functools.partial(_gates_kernel, n_heads=H),
        grid=(M // tm,),
        in_specs=[pl.BlockSpec((tm, D), lambda i: (i, 0)),
                  pl.BlockSpec((2 * H, D), lambda i: (0, 0)),
                  pl.BlockSpec((2 * H, 1), lambda i: (0, 0)),
                  pl.BlockSpec((2 * H, 1), lambda i: (0, 0))],
        out_specs=pl.BlockSpec((2 * H, tm), lambda i: (0, i)),
        out_shape=jax.ShapeDtypeStruct((2 * H, M), F32),
        compiler_params=_params("parallel"),
        name="gates",
    )(u, w_ab_t, col(a_log), col(dt_bias))


def _causal_conv(z_ref, w_ref, n_rows, K):
    base = SUBLANES - (K - 1)
    acc = z_ref[base:base + n_rows, :] * w_ref[0:1, :]
    for k in range(1, K):
        acc = acc + z_ref[base + k:base + k + n_rows, :] * w_ref[k:k + 1, :]
    return acc


def _conv_gate_kernel(cb_ref, cc_ref, cx_ref, hc_ref, hx_ref, w_ref, o_ref, z_ref, *, tiles_per_seq, K):
    first = (pl.program_id(0) % tiles_per_seq) == 0
    n_rows = cb_ref.shape[0]
    halo = hc_ref[...] * hx_ref[...]
    z_ref[0:SUBLANES, :] = halo * jnp.where(first, 0.0, 1.0)
    z_ref[SUBLANES:, :] = cc_ref[...] * cx_ref[...]
    o_ref[...] = (cb_ref[...] * _causal_conv(z_ref, w_ref, n_rows, K)).astype(o_ref.dtype)


def _conv_gate(p, w_t, *, width, seq_len):
    M = p.shape[0]
    K = w_t.shape[0]
    tt = _tile(seq_len, 512, SUBLANES)
    tc = _tile(width, 512)
    nc = width // tc
    hb = tt // SUBLANES
    halo = lambda off: pl.BlockSpec((SUBLANES, tc), lambda i, j: (jnp.maximum(i * hb - 1, 0), j + off))
    return pl.pallas_call(
        functools.partial(_conv_gate_kernel, tiles_per_seq=seq_len // tt, K=K),
        grid=(M // tt, nc),
        in_specs=[pl.BlockSpec((tt, tc), lambda i, j: (i, j)),
                  pl.BlockSpec((tt, tc), lambda i, j: (i, j + nc)),
                  pl.BlockSpec((tt, tc), lambda i, j: (i, j + 2 * nc)),
                  halo(nc), halo(2 * nc),
                  pl.BlockSpec((K, tc), lambda i, j: (0, j))],
        out_specs=pl.BlockSpec((tt, tc), lambda i, j: (i, j)),
        out_shape=jax.ShapeDtypeStruct((M, width), BF16),
        scratch_shapes=[pltpu.VMEM((tt + SUBLANES, tc), F32)],
        compiler_params=_params("parallel", "arbitrary"),
        name="conv_gate",
    )(p, p, p, p, p, w_t)


def _qkv_conv_kernel(x_ref, hx_ref, w_ref, o_ref, z_ref, *, tiles_per_seq, K, n_norm_tiles):
    first = (pl.program_id(0) % tiles_per_seq) == 0
    j = pl.program_id(1)
    n_rows, tc = x_ref.shape
    z_ref[0:SUBLANES, :] = hx_ref[...] * jnp.where(first, 0.0, 1.0)
    z_ref[SUBLANES:, :] = x_ref[...]
    acc = _causal_conv(z_ref, w_ref, n_rows, K)
    y = acc * jax.nn.sigmoid(acc)

    @pl.when(j < n_norm_tiles)
    def _():
        for hh in range(tc // LANES):
            yh = y[:, hh * LANES:(hh + 1) * LANES]
            ss = jnp.sum(yh * yh, axis=-1, keepdims=True)
            o_ref[:, hh * LANES:(hh + 1) * LANES] = (yh * lax.rsqrt(ss + L2_EPS)).astype(o_ref.dtype)

    @pl.when(j >= n_norm_tiles)
    def _():
        o_ref[...] = y.astype(o_ref.dtype)


def _qkv_conv(p, w_t, *, col0, width, seq_len):
    M = p.shape[0]
    K = w_t.shape[0]
    tt = _tile(seq_len, 512, SUBLANES)
    tc = _tile(width, 512)
    nc = width // tc
    off = col0 // tc
    assert col0 % tc == 0
    hb = tt // SUBLANES
    return pl.pallas_call(
        functools.partial(_qkv_conv_kernel, tiles_per_seq=seq_len // tt, K=K, n_norm_tiles=2 * nc),
        grid=(M // tt, 3 * nc),
        in_specs=[pl.BlockSpec((tt, tc), lambda i, j: (i, j + off)),
                  pl.BlockSpec((SUBLANES, tc), lambda i, j: (jnp.maximum(i * hb - 1, 0), j + off)),
                  pl.BlockSpec((K, tc), lambda i, j: (0, j))],
        out_specs=pl.BlockSpec((tt, tc), lambda i, j: (i, j)),
        out_shape=jax.ShapeDtypeStruct((M, 3 * width), BF16),
        scratch_shapes=[pltpu.VMEM((tt + SUBLANES, tc), F32)],
        compiler_params=_params("parallel", "arbitrary"),
        name="qkv_conv",
    )(p, p, w_t)


A_PITCH = CHUNK + SUBLANES


def _delta_kernel(q_ref, k_ref, v_ref, z_ref, g_ref, b_ref, gain_ref, o_ref,
                  gc_ref, s_ref, am_ref, at_ref, tm_ref, u_ref, wq_ref, kt_ref, a_ref, *, n_heads, n_chunks):
    C = CHUNK
    nc = n_heads * n_chunks
    scale = LANES ** -0.5
    row = lax.broadcasted_iota(jnp.int32, (C, C), 0)
    col = lax.broadcasted_iota(jnp.int32, (C, C), 1)
    eye = row == col
    causal = row >= col
    strict = row > col
    zeros_cc = jnp.zeros((C, C), F32)

    @pl.when(pl.program_id(2) == 0)
    def _():
        s_ref[...] = jnp.zeros_like(s_ref)

    prefix = jnp.where(row <= col, 1.0, 0.0)
    for hh in range(n_heads):
        gc_ref[hh] = _dot(g_ref[hh, 0], prefix, precision=HIGHEST)

    def to_col(r):
        return jnp.sum(jnp.where(eye, jnp.broadcast_to(r, (C, C)), 0.0), axis=1, keepdims=True)

    def chunk_slices(hh, c):
        return pl.ds(pl.multiple_of(c * C, C), C), slice(hh * LANES, (hh + 1) * LANES)

    def am_rows(hh, c):
        return pl.ds(pl.multiple_of((hh * n_chunks + c) * A_PITCH, SUBLANES), C)

    def intra(c, carry):
        for hh in range(n_heads):
            rows, lanes = chunk_slices(hh, c)
            k16 = k_ref[rows, lanes]
            k = k16.astype(F32)
            q = q_ref[rows, lanes].astype(F32) * scale
            g_row = gc_ref[hh, pl.ds(c, 1), :]
            g_col = to_col(g_row)
            b_col = to_col(b_ref[hh, 0, pl.ds(c, 1), :])
            decay = jnp.where(causal, jnp.exp(jnp.where(causal, g_col - g_row, 0.0)), 0.0)
            prod = _dot_nt(jnp.concatenate([k * b_col, q], axis=0).astype(BF16), k16)
            a = jnp.where(strict, prod[:C] * decay, 0.0)
            am_ref[am_rows(hh, c), :] = jnp.concatenate([a, zeros_cc], axis=1)
            a_ref[hh, rows, :] = (prod[C:] * decay).astype(BF16)
        return carry

    lax.fori_loop(0, n_chunks, intra, 0)

    for i in range(C):
        at_ref[i] = am_ref[pl.ds(i, nc, stride=A_PITCH), :].T[:C, :]

    zero_blk = jnp.zeros((SUBLANES, nc), F32)
    for i in range(C):
        nb = -(-i // SUBLANES)
        acc = [-at_ref[i, b * SUBLANES:(b + 1) * SUBLANES, :] for b in range(nb)]
        for j in range(1, i):
            a_ij = jnp.broadcast_to(at_ref[i, j:j + 1, :], (SUBLANES, nc))
            for b in range(-(-j // SUBLANES)):
                acc[b] = acc[b] - a_ij * tm_ref[j, b * SUBLANES:(b + 1) * SUBLANES, :]
        for b in range(C // SUBLANES):
            tm_ref[i, b * SUBLANES:(b + 1) * SUBLANES, :] = acc[b] if b < nb else zero_blk

    zeros_pad = jnp.zeros((LANES - C, nc), F32)
    for i in range(C):
        am_ref[pl.ds(i, nc, stride=A_PITCH), :] = jnp.concatenate([tm_ref[i], zeros_pad], axis=0).T

    def solve(c, carry):
        for hh in range(n_heads):
            rows, lanes = chunk_slices(hh, c)
            k = k_ref[rows, lanes].astype(F32)
            v = v_ref[rows, lanes].astype(F32)
            q = q_ref[rows, lanes].astype(F32) * scale
            g_row = gc_ref[hh, pl.ds(c, 1), :]
            g_col = to_col(g_row)
            b_col = to_col(b_ref[hh, 0, pl.ds(c, 1), :])
            e_col = jnp.exp(g_col)
            rhs = jnp.concatenate([v * b_col, k * (b_col * e_col)], axis=1)
            tm = am_ref[am_rows(hh, c), :][:, :C]
            sol = rhs + _dot(tm.astype(BF16), rhs.astype(BF16))
            u_ref[hh, rows, :] = sol[:, :LANES]
            wq_ref[hh, pl.ds(pl.multiple_of(c * 2 * C, 2 * C), C), :] = sol[:, LANES:].astype(BF16)
            wq_ref[hh, pl.ds(pl.multiple_of(c * 2 * C + C, C), C), :] = (q * e_col).astype(BF16)
            k_dec = k * jnp.exp(g_row[:, C - 1:C] - g_col)
            kt_ref[hh, pl.ds(pl.multiple_of(c * LANES, LANES), LANES), :] = k_dec.T.astype(BF16)
        return carry

    lax.fori_loop(0, n_chunks, solve, 0)

    def recur(c, carry):
        heads = range(n_heads)
        rows = pl.ds(pl.multiple_of(c * C, C), C)
        S = [s_ref[hh] for hh in heads]
        r = [_dot(wq_ref[hh, pl.ds(pl.multiple_of(c * 2 * C, 2 * C), 2 * C), :], S[hh].astype(BF16))
             for hh in heads]
        v16 = [(u_ref[hh, rows, :] - r[hh][:C]).astype(BF16) for hh in heads]
        for hh in heads:
            g_last = gc_ref[hh, pl.ds(c, 1), :][:, C - 1:C]
            kt = kt_ref[hh, pl.ds(pl.multiple_of(c * LANES, LANES), LANES), :]
            s_ref[hh] = S[hh] * jnp.exp(g_last) + _dot(kt, v16[hh])
        for hh in heads:
            lanes = slice(hh * LANES, (hh + 1) * LANES)
            o = r[hh][C:] + _dot(a_ref[hh, rows, :], v16[hh])
            z = z_ref[rows, lanes]
            o_ref[rows, lanes] = (_rms(o, gain_ref[...]) * (z * jax.nn.sigmoid(z))).astype(o_ref.dtype)
        return carry

    lax.fori_loop(0, n_chunks, recur, 0)


def _delta(qkv, p, gb, gain, *, z_col0, n_heads, seq_len):
    M = qkv.shape[0]
    W = n_heads * LANES
    B = M // seq_len
    nh = min(n_heads, 8)
    ncb = LANES // nh
    tb = ncb * CHUNK
    hw = nh * LANES
    nb = W // hw
    nt = seq_len // tb
    assert LANES % nh == 0 and n_heads % nh == 0 and seq_len % tb == 0 and z_col0 % hw == 0 and ncb % SUBLANES == 0
    zoff = z_col0 // hw
    tok = lambda off: pl.BlockSpec((tb, hw), lambda b, h, t: (b * nt + t, h + off))
    return pl.pallas_call(
        functools.partial(_delta_kernel, n_heads=nh, n_chunks=ncb),
        grid=(B, nb, nt),
        in_specs=[tok(0), tok(nb), tok(2 * nb), tok(zoff),
                  pl.BlockSpec((nh, 1, ncb, CHUNK), lambda b, h, t: (h, b, t, 0)),
                  pl.BlockSpec((nh, 1, ncb, CHUNK), lambda b, h, t: (h + nb, b, t, 0)),
                  pl.BlockSpec((1, LANES), lambda b, h, t: (0, 0))],
        out_specs=tok(0),
        out_shape=jax.ShapeDtypeStruct((M, W), BF16),
        scratch_shapes=[pltpu.VMEM((nh, ncb, CHUNK), F32),
                        pltpu.VMEM((nh, LANES, LANES), F32),
                        pltpu.VMEM((LANES * A_PITCH, LANES), F32),
                        pltpu.VMEM((CHUNK, CHUNK, LANES), F32),
                        pltpu.VMEM((CHUNK, CHUNK, LANES), F32),
                        pltpu.VMEM((nh, tb, LANES), F32),
                        pltpu.VMEM((nh, 2 * tb, LANES), BF16),
                        pltpu.VMEM((nh, ncb * LANES, CHUNK), BF16),
                        pltpu.VMEM((nh, tb, CHUNK), BF16)],
        compiler_params=_params("parallel", "parallel", "arbitrary"),
        name="delta_rule",
    )(qkv, qkv, qkv, p, gb, gb, gain.reshape(1, LANES))


def _merge_kernel(u_ref, yc_ref, yd_ref, h_ref, wgc_ref, wgd_ref, wc_ref, wdn_ref, wo_ref, o_ref):
    @pl.when(pl.program_id(1) == 0)
    def _():
        def init(rows):
            o_ref[rows, :] = h_ref[rows, :]
        _for_rows(h_ref.shape[0], init)

    u = u_ref[...]
    gc = jax.nn.sigmoid(_dot(u, wgc_ref[...]))
    gd = jax.nn.sigmoid(_dot(u, wgd_ref[...]))
    merged = gc * _dot(yc_ref[...], wc_ref[...]) + gd * _dot(yd_ref[...], wdn_ref[...])
    o_ref[...] += _dot(merged.astype(BF16), wo_ref[...])


def _merge(u, y_conv, y_dn, h, wg, wc, wdn, wo):
    M, D = h.shape
    Wc = y_conv.shape[1]
    Wd = y_dn.shape[1]
    tm = _tile(M, 512, ROW_CHUNK)
    tn = _tile(D, 256)
    nn = D // tn
    row = lambda i, j: (i, 0)
    colw = lambda i, j: (0, j)
    return pl.pallas_call(
        _merge_kernel,
        grid=(M // tm, nn),
        in_specs=[pl.BlockSpec((tm, D), row), pl.BlockSpec((tm, Wc), row), pl.BlockSpec((tm, Wd), row),
                  _single((tm, D), row),
                  pl.BlockSpec((D, tn), colw), pl.BlockSpec((D, tn), lambda i, j: (0, j + nn)),
                  pl.BlockSpec((Wc, tn), colw), pl.BlockSpec((Wd, tn), colw),
                  pl.BlockSpec((tn, D), lambda i, j: (j, 0))],
        out_specs=_single((tm, D), row),
        out_shape=jax.ShapeDtypeStruct((M, D), F32),
        compiler_params=_params("parallel", "arbitrary"),
        name="merge_out",
    )(u, y_conv, y_dn, h, wg, wg, wc, wdn, wo)


def kernel(x, ffn1_norm, ffn1_w_gate, ffn1_w_up, ffn1_w_down, mix_norm, w_in, conv_mixer_w, dn_conv_w, dn_a_log, dn_dt_bias, dn_out_norm, w_conv_branch, w_dn_branch, w_out, ffn2_norm, ffn2_w_gate, ffn2_w_up, ffn2_w_down, final_norm):
    B, T, D = x.shape
    M = B * T
    depth = ffn1_norm.shape[0]
    Wc = conv_mixer_w.shape[1]
    Wd = dn_conv_w.shape[1] // 3
    H = dn_a_log.shape[1]
    assert Wd == H * LANES and dn_out_norm.shape[1] == LANES and T % CHUNK == 0
    c_q = 3 * Wc
    c_z = c_q + 3 * Wd
    c_a = c_z + Wd
    c_gc = c_a + 2 * H
    c_gd = c_gc + D
    assert w_in.shape[2] == c_gd + D
    bf = lambda w: w.astype(BF16)

    h = x.reshape(M, D)
    for l in range(depth):
        last = l == depth - 1
        casts = (w_in[l], w_conv_branch[l], w_dn_branch[l], w_out[l], ffn2_w_gate[l], ffn2_w_up[l], ffn2_w_down[l])
        h, u, w_in16, wc16, wdn16, wo16, wg2, wu2, wd2 = _ffn(
            h, ffn1_norm[l], bf(ffn1_w_gate[l]), bf(ffn1_w_up[l]), bf(ffn1_w_down[l]), mix_norm[l],
            final=False, casts=casts)

        p = _matmul(u, w_in16, c_a, F32)
        gb = _gates(u, w_in16[:, c_a:c_gc].T, dn_a_log[l], dn_dt_bias[l])
        y_conv = _conv_gate(p, conv_mixer_w[l].T, width=Wc, seq_len=T)
        qkv = _qkv_conv(p, dn_conv_w[l].T, col0=c_q, width=Wd, seq_len=T)
        y_dn = _delta(qkv, p, gb.reshape(2 * H, B, T // CHUNK, CHUNK), dn_out_norm[l],
                      z_col0=c_z, n_heads=H, seq_len=T)
        h = _merge(u, y_conv, y_dn, h, w_in16[:, c_gc:], wc16, wdn16, wo16)

        if last:
            h = _ffn(h, ffn2_norm[l], wg2, wu2, wd2, final_norm, final=True)
        else:
            h = _ffn(h, ffn2_norm[l], wg2, wu2, wd2, final_norm, final=False)[0]
    return h.reshape(B, T, D)
```

```python
import functools

import jax
import jax.numpy as jnp
from jax import lax
from jax.experimental import pallas as pl
from jax.experimental.pallas import tpu as pltpu

F32 = jnp.float32
BF16 = jnp.bfloat16
EPS = 1e-6
L2_EPS = 1e-6
CHUNK = 64
LANES = 128
SUBLANES = 8
ROW_CHUNK = 32
FFN_TILES = (512, 256)
V7X_VMEM_LIMIT_BYTES = 56 * 1024 * 1024
HIGHEST = lax.Precision.HIGHEST


def _params(*sem):
    return pltpu.CompilerParams(dimension_semantics=sem, vmem_limit_bytes=V7X_VMEM_LIMIT_BYTES)


def _tile(n, pref, mult=LANES):
    if n <= pref:
        return n
    t = (pref // mult) * mult
    while t > 0 and n % t:
        t -= mult
    assert t > 0, (n, pref, mult)
    return t


def _single(block, index_map):
    return pl.BlockSpec(block, index_map, pipeline_mode=pl.Buffered(1))


def _rms(h, gain):
    return h * lax.rsqrt(jnp.mean(h * h, axis=-1, keepdims=True) + EPS) * gain


def _for_rows(n_rows, fn):
    def body(r, carry):
        fn(pl.ds(pl.multiple_of(r * ROW_CHUNK, ROW_CHUNK), ROW_CHUNK))
        return carry
    lax.fori_loop(0, n_rows // ROW_CHUNK, body, 0)


def _dot(a, b, **kw):
    return jnp.dot(a, b, preferred_element_type=F32, **kw)


def _dot_nt(a, b, **kw):
    return lax.dot_general(a, b, (((1,), (1,)), ((), ())), preferred_element_type=F32, **kw)


def _dot_tn(a, b, **kw):
    return lax.dot_general(a, b, (((0,), (0,)), ((), ())), preferred_element_type=F32, **kw)


def _spread(shape, grid):
    R, Cn = shape
    gi, gj = grid
    best = None
    for shift in range(12):
        bc = -(-Cn // (LANES << shift)) * LANES
        ncol = -(-Cn // bc)
        if ncol != 1 << shift:
            continue
        nr_max = (gi * gj) // ncol
        if nr_max == 0:
            break
        br = -(-R // (nr_max * 2 * SUBLANES)) * 2 * SUBLANES
        nr = -(-R // br)
        key = (R % br == 0 and Cn % bc == 0 and 2 * nr * ncol >= gi * gj, nr * ncol)
        if best is None or key > best[0]:
            best = (key, br, bc, nr, shift)
    _, br, bc, nr, shift = best

    def index_map(i, j):
        s = jnp.minimum(i * gj + j, (nr << shift) - 1)
        return s >> shift, s & ((1 << shift) - 1)
    return (br, bc), index_map


def _ffn_kernel(h_ref, g_ref, wg_ref, wu_ref, wd_ref, ng_ref, *refs, final, n_casts, last_cols):
    cast_in, refs = refs[:n_casts], refs[n_casts:]
    if final:
        o_ref, xn_ref = refs
    else:
        o_ref, u_ref, *cast_out, xn_ref = refs
        for src, dst in zip(cast_in, cast_out):
            dst[...] = src[...].astype(BF16)
    j = pl.program_id(1)
    tm = h_ref.shape[0]

    @pl.when(j == 0)
    def _():
        def init(rows):
            h = h_ref[rows, :]
            xn_ref[rows, :] = _rms(h, g_ref[...]).astype(BF16)
            o_ref[rows, :] = h
        _for_rows(tm, init)

    def ff_tile(n_cols):
        xn = xn_ref[...]
        gate = _dot(xn, wg_ref[:, :n_cols])
        up = _dot(xn, wu_ref[:, :n_cols])
        act = (gate * jax.nn.sigmoid(gate) * (0.5 * up)).astype(BF16)
        o_ref[...] += _dot(act, wd_ref[:n_cols, :])

    last = pl.num_programs(1) - 1
    if last_cols == wg_ref.shape[1]:
        ff_tile(last_cols)
    else:
        pl.when(j < last)(lambda: ff_tile(wg_ref.shape[1]))
        pl.when(j == last)(lambda: ff_tile(last_cols))

    @pl.when(j == last)
    def _():
        def fin(rows):
            y = _rms(o_ref[rows, :], ng_ref[...])
            if final:
                o_ref[rows, :] = y
            else:
                u_ref[rows, :] = y.astype(BF16)
        _for_rows(tm, fin)


def _ffn(h, gain, wg, wu, wd, next_gain, *, final, casts=()):
    M, D = h.shape
    F = wg.shape[1]
    tm = _tile(M, 512, ROW_CHUNK)

    def vmem_bytes(tf):
        steps = (M // tm) * -(-F // tf)
        rows = tm * D * (4 + 4 + 2 + (0 if final else 2))
        side = 0 if final else sum(2 * 6 * -(-a.size // steps) for a in casts)
        return rows + 2 * 3 * D * tf * 2 + side + 3 * tm * tf * 4

    tf = next((t for t in FFN_TILES if vmem_bytes(min(F, t)) <= V7X_VMEM_LIMIT_BYTES), FFN_TILES[-1])
    tf = min(F, tf)
    nj = -(-F // tf)
    last_cols = F - (nj - 1) * tf
    assert last_cols % LANES == 0
    grid = (M // tm, nj)
    row = lambda i, j: (i, 0)
    out_shape = [jax.ShapeDtypeStruct((M, D), F32)]
    out_specs = [_single((tm, D), row)]
    cast_specs = []
    if not final:
        out_shape.append(jax.ShapeDtypeStruct((M, D), BF16))
        out_specs.append(_single((tm, D), row))
        for arr in casts:
            cast_specs.append(pl.BlockSpec(*_spread(arr.shape, grid)))
            out_shape.append(jax.ShapeDtypeStruct(arr.shape, BF16))
        out_specs += cast_specs
    res = pl.pallas_call(
        functools.partial(_ffn_kernel, final=final, n_casts=len(cast_specs), last_cols=last_cols),
        grid=grid,
        in_specs=[
            _single((tm, D), row),
            pl.BlockSpec((1, D), lambda i, j: (0, 0)),
            pl.BlockSpec((D, tf), lambda i, j: (0, j)),
            pl.BlockSpec((D, tf), lambda i, j: (0, j)),
            pl.BlockSpec((tf, D), lambda i, j: (j, 0)),
            pl.BlockSpec((1, D), lambda i, j: (0, 0)),
        ] + cast_specs,
        out_specs=out_specs,
        out_shape=out_shape,
        scratch_shapes=[pltpu.VMEM((tm, D), BF16)],
        compiler_params=_params("parallel", "arbitrary"),
        name="ffn_final" if final else "ffn",
    )(h, gain.reshape(1, D), wg, wu, wd, next_gain.reshape(1, D), *casts)
    return res[0] if final else res


def _matmul_kernel(a_ref, b_ref, o_ref):
    o_ref[...] = _dot_nt(a_ref[...], b_ref[...]).astype(o_ref.dtype)


def _matmul(a, b_t, n_cols, out_dtype):
    M, K = a.shape
    N = n_cols
    tm = _tile(M, 1024, SUBLANES)
    tn = _tile(N, 512)
    return pl.pallas_call(
        _matmul_kernel,
        grid=(M // tm, N // tn),
        in_specs=[pl.BlockSpec((tm, K), lambda i, j: (i, 0)),
                  pl.BlockSpec((tn, K), lambda i, j: (j, 0))],
        out_specs=pl.BlockSpec((tm, tn), lambda i, j: (i, j)),
        out_shape=jax.ShapeDtypeStruct((M, N), out_dtype),
        compiler_params=_params("parallel", "arbitrary"),
        name="in_proj",
    )(a, b_t)


def _gates_kernel(u_ref, w_ref, alog_ref, bias_ref, o_ref, *, n_heads):
    x = _dot_nt(w_ref[...], u_ref[...])
    s = x + bias_ref[...]
    softplus = jnp.maximum(s, 0.0) + jnp.log1p(jnp.exp(-jnp.abs(s)))
    g = -jnp.exp(alog_ref[...]) * softplus
    row = lax.broadcasted_iota(jnp.int32, x.shape, 0)
    o_ref[...] = jnp.where(row < n_heads, g, jax.nn.sigmoid(x))


def _gates(u, w_ab_t, a_log, dt_bias):
    M, D = u.shape
    H = a_log.shape[0]
    tm = _tile(M, 512)
    pad = jnp.zeros((H,), F32)
    col = lambda v: jnp.concatenate([v.astype(F32), pad]).reshape(2 * H, 1)
    return pl.pallas_call(
        functools.partial(_gates_kernel, n_heads=H),
        grid=(M // tm,),
        in_specs=[pl.BlockSpec((tm, D), lambda i: (i, 0)),
                  pl.BlockSpec((2 * H, D), lambda i: (0, 0)),
                  pl.BlockSpec((2 * H, 1), lambda i: (0, 0)),
                  pl.BlockSpec((2 * H, 1), lambda i: (0, 0))],
        out_specs=pl.BlockSpec((2 * H, tm), lambda i: (0, i)),
        out_shape=jax.ShapeDtypeStruct((2 * H, M), F32),
        compiler_params=_params("parallel"),
        name="gates",
    )(u, w_ab_t, col(a_log), col(dt_bias))


def _causal_conv(z_ref, w_ref, n_rows, K):
    base = SUBLANES - (K - 1)
    acc = z_ref[base:base + n_rows, :] * w_ref[0:1, :]
    for k in range(1, K):
        acc = acc + z_ref[base + k:base + k + n_rows, :] * w_ref[k:k + 1, :]
    return acc


def _conv_gate_kernel(cb_ref, cc_ref, cx_ref, hc_ref, hx_ref, w_ref, o_ref, z_ref, *, tiles_per_seq, K):
    first = (pl.program_id(0) % tiles_per_seq) == 0
    n_rows = cb_ref.shape[0]
    halo = hc_ref[...] * hx_ref[...]
    z_ref[0:SUBLANES, :] = halo * jnp.where(first, 0.0, 1.0)
    z_ref[SUBLANES:, :] = cc_ref[...] * cx_ref[...]
    o_ref[...] = (cb_ref[...] * _causal_conv(z_ref, w_ref, n_rows, K)).astype(o_ref.dtype)


def _conv_gate(p, w_t, *, width, seq_len):
    M = p.shape[0]
    K = w_t.shape[0]
    tt = _tile(seq_len, 512, SUBLANES)
    tc = _tile(width, 512)
    nc = width // tc
    hb = tt // SUBLANES
    halo = lambda off: pl.BlockSpec((SUBLANES, tc), lambda i, j: (jnp.maximum(i * hb - 1, 0), j + off))
    return pl.pallas_call(
        functools.partial(_conv_gate_kernel, tiles_per_seq=seq_len // tt, K=K),
        grid=(M // tt, nc),
        in_specs=[pl.BlockSpec((tt, tc), lambda i, j: (i, j)),
                  pl.BlockSpec((tt, tc), lambda i, j: (i, j + nc)),
                  pl.BlockSpec((tt, tc), lambda i, j: (i, j + 2 * nc)),
                  halo(nc), halo(2 * nc),
                  pl.BlockSpec((K, tc), lambda i, j: (0, j))],
        out_specs=pl.BlockSpec((tt, tc), lambda i, j: (i, j)),
        out_shape=jax.ShapeDtypeStruct((M, width), BF16),
        scratch_shapes=[pltpu.VMEM((tt + SUBLANES, tc), F32)],
        compiler_params=_params("parallel", "arbitrary"),
        name="conv_gate",
    )(p, p, p, p, p, w_t)


def _qkv_conv_kernel(x_ref, hx_ref, w_ref, o_ref, z_ref, *, tiles_per_seq, K, n_norm_tiles):
    first = (pl.program_id(0) % tiles_per_seq) == 0
    j = pl.program_id(1)
    n_rows, tc = x_ref.shape
    z_ref[0:SUBLANES, :] = hx_ref[...] * jnp.where(first, 0.0, 1.0)
    z_ref[SUBLANES:, :] = x_ref[...]
    acc = _causal_conv(z_ref, w_ref, n_rows, K)
    y = acc * jax.nn.sigmoid(acc)

    @pl.when(j < n_norm_tiles)
    def _():
        for hh in range(tc // LANES):
            yh = y[:, hh * LANES:(hh + 1) * LANES]
            ss = jnp.sum(yh * yh, axis=-1, keepdims=True)
            o_ref[:, hh * LANES:(hh + 1) * LANES] = (yh * lax.rsqrt(ss + L2_EPS)).astype(o_ref.dtype)

    @pl.when(j >= n_norm_tiles)
    def _():
        o_ref[...] = y.astype(o_ref.dtype)


def _qkv_conv(p, w_t, *, col0, width, seq_len):
    M = p.shape[0]
    K = w_t.shape[0]
    tt = _tile(seq_len, 512, SUBLANES)
    tc = _tile(width, 512)
    nc = width // tc
    off = col0 // tc
    assert col0 % tc == 0
    hb = tt // SUBLANES
    return pl.pallas_call(
        functools.partial(_qkv_conv_kernel, tiles_per_seq=seq_len // tt, K=K, n_norm_tiles=2 * nc),
        grid=(M // tt, 3 * nc),
        in_specs=[pl.BlockSpec((tt, tc), lambda i, j: (i, j + off)),
                  pl.BlockSpec((SUBLANES, tc), lambda i, j: (jnp.maximum(i * hb - 1, 0), j + off)),
                  pl.BlockSpec((K, tc), lambda i, j: (0, j))],
        out_specs=pl.BlockSpec((tt, tc), lambda i, j: (i, j)),
        out_shape=jax.ShapeDtypeStruct((M, 3 * width), BF16),
        scratch_shapes=[pltpu.VMEM((tt + SUBLANES, tc), F32)],
        compiler_params=_params("parallel", "arbitrary"),
        name="qkv_conv",
    )(p, p, w_t)


A_PITCH = CHUNK + SUBLANES


def _delta_kernel(q_ref, k_ref, v_ref, z_ref, g_ref, b_ref, gain_ref, o_ref,
                  gc_ref, s_ref, am_ref, at_ref, tm_ref, u_ref, wq_ref, kt_ref, a_ref, *, n_heads, n_chunks):
    C = CHUNK
    nc = n_heads * n_chunks
    scale = LANES ** -0.5
    row = lax.broadcasted_iota(jnp.int32, (C, C), 0)
    col = lax.broadcasted_iota(jnp.int32, (C, C), 1)
    eye = row == col
    causal = row >= col
    strict = row > col
    zeros_cc = jnp.zeros((C, C), F32)

    @pl.when(pl.program_id(2) == 0)
    def _():
        s_ref[...] = jnp.zeros_like(s_ref)

    prefix = jnp.where(row <= col, 1.0, 0.0)
    for hh in range(n_heads):
        gc_ref[hh] = _dot(g_ref[hh, 0], prefix, precision=HIGHEST)

    def to_col(r):
        return jnp.sum(jnp.where(eye, jnp.broadcast_to(r, (C, C)), 0.0), axis=1, keepdims=True)

    def chunk_slices(hh, c):
        return pl.ds(pl.multiple_of(c * C, C), C), slice(hh * LANES, (hh + 1) * LANES)

    def am_rows(hh, c):
        return pl.ds(pl.multiple_of((hh * n_chunks + c) * A_PITCH, SUBLANES), C)

    def intra(c, carry):
        for hh in range(n_heads):
            rows, lanes = chunk_slices(hh, c)
            k16 = k_ref[rows, lanes]
            k = k16.astype(F32)
            q = q_ref[rows, lanes].astype(F32) * scale
            g_row = gc_ref[hh, pl.ds(c, 1), :]
            g_col = to_col(g_row)
            b_col = to_col(b_ref[hh, 0, pl.ds(c, 1), :])
            decay = jnp.where(causal, jnp.exp(jnp.where(causal, g_col - g_row, 0.0)), 0.0)
            prod = _dot_nt(jnp.concatenate([k * b_col, q], axis=0).astype(BF16), k16)
            a = jnp.where(strict, prod[:C] * decay, 0.0)
            am_ref[am_rows(hh, c), :] = jnp.concatenate([a, zeros_cc], axis=1)
            a_ref[hh, rows, :] = (prod[C:] * decay).astype(BF16)
        return carry

    lax.fori_loop(0, n_chunks, intra, 0)

    for i in range(C):
        at_ref[i] = am_ref[pl.ds(i, nc, stride=A_PITCH), :].T[:C, :]

    zero_blk = jnp.zeros((SUBLANES, nc), F32)
    for i in range(C):
        nb = -(-i // SUBLANES)
        acc = [-at_ref[i, b * SUBLANES:(b + 1) * SUBLANES, :] for b in range(nb)]
        for j in range(1, i):
            a_ij = jnp.broadcast_to(at_ref[i, j:j + 1, :], (SUBLANES, nc))
            for b in range(-(-j // SUBLANES)):
                acc[b] = acc[b] - a_ij * tm_ref[j, b * SUBLANES:(b + 1) * SUBLANES, :]
        for b in range(C // SUBLANES):
            tm_ref[i, b * SUBLANES:(b + 1) * SUBLANES, :] = acc[b] if b < nb else zero_blk

    zeros_pad = jnp.zeros((LANES - C, nc), F32)
    for i in range(C):
        am_ref[pl.ds(i, nc, stride=A_PITCH), :] = jnp.concatenate([tm_ref[i], zeros_pad], axis=0).T

    def solve(c, carry):
        for hh in range(n_heads):
            rows, lanes = chunk_slices(hh, c)
            k = k_ref[rows, lanes].astype(F32)
            v = v_ref[rows, lanes].astype(F32)
            q = q_ref[rows, lanes].astype(F32) * scale
            g_row = gc_ref[hh, pl.ds(c, 1), :]
            g_col = to_col(g_row)
            b_col = to_col(b_ref[hh, 0, pl.ds(c, 1), :])
            e_col = jnp.exp(g_col)
            rhs = jnp.concatenate([v * b_col, k * (b_col * e_col)], axis=1)
            tm = am_ref[am_rows(hh, c), :][:, :C]
            sol = rhs + _dot(tm.astype(BF16), rhs.astype(BF16))
            u_ref[hh, rows, :] = sol[:, :LANES]
            wq_ref[hh, pl.ds(pl.multiple_of(c * 2 * C, 2 * C), C), :] = sol[:, LANES:].astype(BF16)
            wq_ref[hh, pl.ds(pl.multiple_of(c * 2 * C + C, C), C), :] = (q * e_col).astype(BF16)
            k_dec = k * jnp.exp(g_row[:, C - 1:C] - g_col)
            kt_ref[hh, pl.ds(pl.multiple_of(c * LANES, LANES), LANES), :] = k_dec.T.astype(BF16)
        return carry

    lax.fori_loop(0, n_chunks, solve, 0)

    def recur(c, carry):
        heads = range(n_heads)
        rows = pl.ds(pl.multiple_of(c * C, C), C)
        S = [s_ref[hh] for hh in heads]
        r = [_dot(wq_ref[hh, pl.ds(pl.multiple_of(c * 2 * C, 2 * C), 2 * C), :], S[hh].astype(BF16))
             for hh in heads]
        v16 = [(u_ref[hh, rows, :] - r[hh][:C]).astype(BF16) for hh in heads]
        for hh in heads:
            g_last = gc_ref[hh, pl.ds(c, 1), :][:, C - 1:C]
            kt = kt_ref[hh, pl.ds(pl.multiple_of(c * LANES, LANES), LANES), :]
            s_ref[hh] = S[hh] * jnp.exp(g_last) + _dot(kt, v16[hh])
        for hh in heads:
            lanes = slice(hh * LANES, (hh + 1) * LANES)
            o = r[hh][C:] + _dot(a_ref[hh, rows, :], v16[hh])
            z = z_ref[rows, lanes]
            o_ref[rows, lanes] = (_rms(o, gain_ref[...]) * (z * jax.nn.sigmoid(z))).astype(o_ref.dtype)
        return carry

    lax.fori_loop(0, n_chunks, recur, 0)


def _delta(qkv, p, gb, gain, *, z_col0, n_heads, seq_len):
    M = qkv.shape[0]
    W = n_heads * LANES
    B = M // seq_len
    nh = min(n_heads, 8)
    ncb = LANES // nh
    tb = ncb * CHUNK
    hw = nh * LANES
    nb = W // hw
    nt = seq_len // tb
    assert LANES % nh == 0 and n_heads % nh == 0 and seq_len % tb == 0 and z_col0 % hw == 0 and ncb % SUBLANES == 0
    zoff = z_col0 // hw
    tok = lambda off: pl.BlockSpec((tb, hw), lambda b, h, t: (b * nt + t, h + off))
    return pl.pallas_call(
        functools.partial(_delta_kernel, n_heads=nh, n_chunks=ncb),
        grid=(B, nb, nt),
        in_specs=[tok(0), tok(nb), tok(2 * nb), tok(zoff),
                  pl.BlockSpec((nh, 1, ncb, CHUNK), lambda b, h, t: (h, b, t, 0)),
                  pl.BlockSpec((nh, 1, ncb, CHUNK), lambda b, h, t: (h + nb, b, t, 0)),
                  pl.BlockSpec((1, LANES), lambda b, h, t: (0, 0))],
        out_specs=tok(0),
        out_shape=jax.ShapeDtypeStruct((M, W), BF16),
        scratch_shapes=[pltpu.VMEM((nh, ncb, CHUNK), F32),
                        pltpu.VMEM((nh, LANES, LANES), F32),
                        pltpu.VMEM((LANES * A_PITCH, LANES), F32),
                        pltpu.VMEM((CHUNK, CHUNK, LANES), F32),
                        pltpu.VMEM((CHUNK, CHUNK, LANES), F32),
                        pltpu.VMEM((nh, tb, LANES), F32),
                        pltpu.VMEM((nh, 2 * tb, LANES), BF16),
                        pltpu.VMEM((nh, ncb * LANES, CHUNK), BF16),
                        pltpu.VMEM((nh, tb, CHUNK), BF16)],
        compiler_params=_params("parallel", "parallel", "arbitrary"),
        name="delta_rule",
    )(qkv, qkv, qkv, p, gb, gb, gain.reshape(1, LANES))


def _merge_kernel(u_ref, yc_ref, yd_ref, h_ref, wgc_ref, wgd_ref, wc_ref, wdn_ref, wo_ref, o_ref):
    @pl.when(pl.program_id(1) == 0)
    def _():
        def init(rows):
            o_ref[rows, :] = h_ref[rows, :]
        _for_rows(h_ref.shape[0], init)

    u = u_ref[...]
    gc = jax.nn.sigmoid(_dot_nt(u, wgc_ref[...]))
    gd = jax.nn.sigmoid(_dot_nt(u, wgd_ref[...]))
    merged = gc * _dot(yc_ref[...], wc_ref[...]) + gd * _dot(yd_ref[...], wdn_ref[...])
    o_ref[...] += _dot(merged.astype(BF16), wo_ref[...])


def _merge(u, y_conv, y_dn, h, wg_t, wc, wdn, wo):
    M, D = h.shape
    Wc = y_conv.shape[1]
    Wd = y_dn.shape[1]
    tm = _tile(M, 512, ROW_CHUNK)
    tn = _tile(D, 256)
    nn = D // tn
    row = lambda i, j: (i, 0)
    colw = lambda i, j: (0, j)
    return pl.pallas_call(
        _merge_kernel,
        grid=(M // tm, nn),
        in_specs=[pl.BlockSpec((tm, D), row), pl.BlockSpec((tm, Wc), row), pl.BlockSpec((tm, Wd), row),
                  _single((tm, D), row),
                  pl.BlockSpec((tn, D), lambda i, j: (j, 0)), pl.BlockSpec((tn, D), lambda i, j: (j + nn, 0)),
                  pl.BlockSpec((Wc, tn), colw), pl.BlockSpec((Wd, tn), colw),
                  pl.BlockSpec((tn, D), lambda i, j: (j, 0))],
        out_specs=_single((tm, D), row),
        out_shape=jax.ShapeDtypeStruct((M, D), F32),
        compiler_params=_params("parallel", "arbitrary"),
        name="merge_out",
    )(u, y_conv, y_dn, h, wg_t, wg_t, wc, wdn, wo)


def kernel(x, ffn1_norm, ffn1_w_gate, ffn1_w_up, ffn1_w_down, mix_norm, w_in, conv_mixer_w, dn_conv_w, dn_a_log, dn_dt_bias, dn_out_norm, w_conv_branch, w_dn_branch, w_out, ffn2_norm, ffn2_w_gate, ffn2_w_up, ffn2_w_down, final_norm):
    B, T, D = x.shape
    M = B * T
    depth = ffn1_norm.shape[0]
    Wc = conv_mixer_w.shape[1]
    Wd = dn_conv_w.shape[1] // 3
    H = dn_a_log.shape[1]
    assert Wd == H * LANES and dn_out_norm.shape[1] == LANES and T % CHUNK == 0
    c_q = 3 * Wc
    c_z = c_q + 3 * Wd
    c_a = c_z + Wd
    c_gc = c_a + 2 * H
    c_gd = c_gc + D
    assert w_in.shape[2] == c_gd + D
    bf = lambda w: w.astype(BF16)

    h = x.reshape(M, D)
    for l in range(depth):
        last = l == depth - 1
        casts = (w_in[l].T, w_conv_branch[l], w_dn_branch[l], w_out[l], ffn2_w_gate[l], ffn2_w_up[l], ffn2_w_down[l])
        h, u, w_in16, wc16, wdn16, wo16, wg2, wu2, wd2 = _ffn(
            h, ffn1_norm[l], bf(ffn1_w_gate[l]), bf(ffn1_w_up[l]), bf(ffn1_w_down[l]), mix_norm[l],
            final=False, casts=casts)

        p = _matmul(u, w_in16, c_a, F32)
        gb = _gates(u, w_in16[c_a:c_gc], dn_a_log[l], dn_dt_bias[l])
        y_conv = _conv_gate(p, conv_mixer_w[l].T, width=Wc, seq_len=T)
        qkv = _qkv_conv(p, dn_conv_w[l].T, col0=c_q, width=Wd, seq_len=T)
        y_dn = _delta(qkv, p, gb.reshape(2 * H, B, T // CHUNK, CHUNK), dn_out_norm[l],
                      z_col0=c_z, n_heads=H, seq_len=T)
        h = _merge(u, y_conv, y_dn, h, w_in16[c_gc:], wc16, wdn16, wo16)

        if last:
            h = _ffn(h, ffn2_norm[l], wg2, wu2, wd2, final_norm, final=True)
        else:
            h = _ffn(h, ffn2_norm[l], wg2, wu2, wd2, final_norm, final=False)[0]
    return h.reshape(B, T, D)
```

```python
import functools

import jax
import jax.numpy as jnp
from jax import lax
from jax.experimental import pallas as pl
from jax.experimental.pallas import tpu as pltpu

F32 = jnp.float32
BF16 = jnp.bfloat16
EPS = 1e-6
L2_EPS = 1e-6
CHUNK = 64
LANES = 128
SUBLANES = 8
MXU_COLS = 256
ROW_CHUNK = 32
FFN_TILES = (512, 256)
V7X_VMEM_LIMIT_BYTES = 56 * 1024 * 1024
HIGHEST = lax.Precision.HIGHEST


def _params(*sem):
    return pltpu.CompilerParams(dimension_semantics=sem, vmem_limit_bytes=V7X_VMEM_LIMIT_BYTES)


def _tile(n, pref, mult=LANES):
    if n <= pref:
        return n
    t = (pref // mult) * mult
    while t > 0 and n % t:
        t -= mult
    assert t > 0, (n, pref, mult)
    return t


def _single(block, index_map):
    return pl.BlockSpec(block, index_map, pipeline_mode=pl.Buffered(1))


def _rms(h, gain):
    return h * lax.rsqrt(jnp.mean(h * h, axis=-1, keepdims=True) + EPS) * gain


def _for_rows(n_rows, fn):
    def body(r, carry):
        fn(pl.ds(pl.multiple_of(r * ROW_CHUNK, ROW_CHUNK), ROW_CHUNK))
        return carry
    lax.fori_loop(0, n_rows // ROW_CHUNK, body, 0)


def _dot(a, b, **kw):
    return jnp.dot(a, b, preferred_element_type=F32, **kw)


def _dot_nt(a, b, **kw):
    return lax.dot_general(a, b, (((1,), (1,)), ((), ())), preferred_element_type=F32, **kw)


def _dot_tn(a, b, **kw):
    return lax.dot_general(a, b, (((0,), (0,)), ((), ())), preferred_element_type=F32, **kw)


def _spread(shape, grid):
    R, Cn = shape
    gi, gj = grid
    best = None
    for shift in range(12):
        bc = -(-Cn // (LANES << shift)) * LANES
        ncol = -(-Cn // bc)
        if ncol != 1 << shift:
            continue
        nr_max = (gi * gj) // ncol
        if nr_max == 0:
            break
        br = -(-R // (nr_max * 2 * SUBLANES)) * 2 * SUBLANES
        nr = -(-R // br)
        key = (R % br == 0 and Cn % bc == 0 and 2 * nr * ncol >= gi * gj, nr * ncol)
        if best is None or key > best[0]:
            best = (key, br, bc, nr, shift)
    _, br, bc, nr, shift = best

    def index_map(i, j):
        s = jnp.minimum(i * gj + j, (nr << shift) - 1)
        return s >> shift, s & ((1 << shift) - 1)
    return (br, bc), index_map


def _ffn_kernel(h_ref, g_ref, wg_ref, wu_ref, wd_ref, ng_ref, *refs, final, n_casts, last_cols):
    cast_in, refs = refs[:n_casts], refs[n_casts:]
    if final:
        o_ref, xn_ref = refs
    else:
        o_ref, u_ref, *cast_out, xn_ref = refs
        for src, dst in zip(cast_in, cast_out):
            dst[...] = src[...].astype(BF16)
    j = pl.program_id(1)

    @pl.when(j == 0)
    def _():
        def init(rows):
            h = h_ref[rows, :]
            xn_ref[rows, :] = _rms(h, g_ref[...]).astype(BF16)
            o_ref[rows, :] = h
        _for_rows(h_ref.shape[0], init)

    def ff_tile(n_cols):
        xn = xn_ref[...]
        parts = [(c, min(c + MXU_COLS, n_cols)) for c in range(0, n_cols, MXU_COLS)]
        gate_up = [(_dot(xn, wg_ref[:, a:b]), _dot(xn, wu_ref[:, a:b])) for a, b in parts]
        for (a, b), (gate, up) in zip(parts, gate_up):
            act = (gate * jax.nn.sigmoid(gate) * (0.5 * up)).astype(BF16)
            o_ref[...] += _dot(act, wd_ref[a:b, :])

    last = pl.num_programs(1) - 1
    if last_cols == wg_ref.shape[1]:
        ff_tile(last_cols)
    else:
        pl.when(j < last)(lambda: ff_tile(wg_ref.shape[1]))
        pl.when(j == last)(lambda: ff_tile(last_cols))

    @pl.when(j == last)
    def _():
        def fin(rows):
            y = _rms(o_ref[rows, :], ng_ref[...])
            if final:
                o_ref[rows, :] = y
            else:
                u_ref[rows, :] = y.astype(BF16)
        _for_rows(h_ref.shape[0], fin)


def _ffn(h, gain, wg, wu, wd, next_gain, *, final, casts=()):
    M, D = h.shape
    F = wg.shape[1]
    tm = _tile(M, 512, ROW_CHUNK)

    def vmem_bytes(tf):
        steps = (M // tm) * -(-F // tf)
        rows = tm * D * (4 + 4 + 2 + (0 if final else 2))
        side = 0 if final else sum(2 * 6 * -(-a.size // steps) for a in casts)
        return rows + 2 * 3 * D * tf * 2 + side + 3 * tm * tf * 4

    tf = next((t for t in FFN_TILES if vmem_bytes(min(F, t)) <= V7X_VMEM_LIMIT_BYTES), FFN_TILES[-1])
    tf = min(F, tf)
    nj = -(-F // tf)
    last_cols = F - (nj - 1) * tf
    assert last_cols % LANES == 0
    grid = (M // tm, nj)
    row = lambda i, j: (i, 0)
    out_shape = [jax.ShapeDtypeStruct((M, D), F32)]
    out_specs = [_single((tm, D), row)]
    cast_specs = []
    if not final:
        out_shape.append(jax.ShapeDtypeStruct((M, D), BF16))
        out_specs.append(_single((tm, D), row))
        for arr in casts:
            cast_specs.append(pl.BlockSpec(*_spread(arr.shape, grid)))
            out_shape.append(jax.ShapeDtypeStruct(arr.shape, BF16))
        out_specs += cast_specs
    res = pl.pallas_call(
        functools.partial(_ffn_kernel, final=final, n_casts=len(cast_specs), last_cols=last_cols),
        grid=grid,
        in_specs=[
            _single((tm, D), row),
            pl.BlockSpec((1, D), lambda i, j: (0, 0)),
            pl.BlockSpec((D, tf), lambda i, j: (0, j)),
            pl.BlockSpec((D, tf), lambda i, j: (0, j)),
            pl.BlockSpec((tf, D), lambda i, j: (j, 0)),
            pl.BlockSpec((1, D), lambda i, j: (0, 0)),
        ] + cast_specs,
        out_specs=out_specs,
        out_shape=out_shape,
        scratch_shapes=[pltpu.VMEM((tm, D), BF16)],
        compiler_params=_params("parallel", "arbitrary"),
        name="ffn_final" if final else "ffn",
    )(h, gain.reshape(1, D), wg, wu, wd, next_gain.reshape(1, D), *casts)
    return res[0] if final else res


CONV_ROWS = 64


def _conv_piece(z_ref, w_ref, K, r0, lanes):
    zz = z_ref[pl.ds(r0, CONV_ROWS + SUBLANES), lanes]
    acc = zz[SUBLANES:, :] * w_ref[K - 1:K, lanes]
    for s in range(1, K):
        acc = acc + pltpu.roll(zz, s, axis=0)[SUBLANES:, :] * w_ref[K - 1 - s:K - s, lanes]
    return acc


def _load_halo(z_ref, halo_ref, j, first):
    @pl.when(first)
    def _():
        z_ref[0:SUBLANES, :] = jnp.zeros((SUBLANES, z_ref.shape[1]), F32)

    @pl.when(jnp.logical_not(first))
    def _():
        z_ref[0:SUBLANES, :] = halo_ref[j]


def _proj_gate_kernel(u_ref, wb_ref, wc_ref, wx_ref, cw_ref, o_ref, z_ref, halo_ref, *, tiles_per_seq, K):
    i, j = pl.program_id(0), pl.program_id(1)
    tm, tn = o_ref.shape
    _load_halo(z_ref, halo_ref, j, (i % tiles_per_seq) == 0)
    u = u_ref[...]
    z_ref[SUBLANES:, :] = _dot_nt(u, wc_ref[...]) * _dot_nt(u, wx_ref[...])
    halo_ref[j] = z_ref[tm:tm + SUBLANES, :]
    b = _dot_nt(u, wb_ref[...])
    for r0 in range(0, tm, CONV_ROWS):
        for c0 in range(0, tn, LANES):
            lanes = slice(c0, c0 + LANES)
            y = b[r0:r0 + CONV_ROWS, lanes] * _conv_piece(z_ref, cw_ref, K, r0, lanes)
            o_ref[r0:r0 + CONV_ROWS, lanes] = y.astype(o_ref.dtype)


def _proj_gate(u, w_t, conv_w, *, width, seq_len):
    M, D = u.shape
    K = conv_w.shape[0]
    tm = _tile(seq_len, 1024, CONV_ROWS)
    tn = _tile(width, MXU_COLS)
    nj = width // tn
    wspec = lambda off: pl.BlockSpec((tn, D), lambda i, j: (j + off, 0))
    return pl.pallas_call(
        functools.partial(_proj_gate_kernel, tiles_per_seq=seq_len // tm, K=K),
        grid=(M // tm, nj),
        in_specs=[pl.BlockSpec((tm, D), lambda i, j: (i, 0)), wspec(0), wspec(nj), wspec(2 * nj),
                  pl.BlockSpec((K, tn), lambda i, j: (0, j))],
        out_specs=pl.BlockSpec((tm, tn), lambda i, j: (i, j)),
        out_shape=jax.ShapeDtypeStruct((M, width), BF16),
        scratch_shapes=[pltpu.VMEM((tm + SUBLANES, tn), F32), pltpu.VMEM((nj, SUBLANES, tn), F32)],
        compiler_params=_params("arbitrary", "arbitrary"),
        name="proj_gate",
    )(u, w_t, w_t, w_t, conv_w)


def _proj_conv_kernel(u_ref, w_ref, cw_ref, o_ref, z_ref, halo_ref, *, nj, tiles_per_seq, K, n_norm_tiles):
    s = pl.program_id(0)
    tm, tn = o_ref.shape

    @pl.when(s == 0)
    def _():
        z_ref[...] = jnp.zeros_like(z_ref)

    t = jnp.maximum(s - 1, 0)
    i, j = t // nj, t % nj
    _load_halo(z_ref, halo_ref, j, (i % tiles_per_seq) == 0)
    normed = j < n_norm_tiles
    for r0 in range(0, tm, CONV_ROWS):
        for c0 in range(0, tn, LANES):
            lanes = slice(c0, c0 + LANES)
            acc = _conv_piece(z_ref, cw_ref, K, r0, lanes)
            y = acc * jax.nn.sigmoid(acc)
            ss = jnp.sum(y * y, axis=-1, keepdims=True)
            y = y * jnp.where(normed, lax.rsqrt(ss + L2_EPS), 1.0)
            o_ref[r0:r0 + CONV_ROWS, lanes] = y.astype(o_ref.dtype)
    halo_ref[j] = z_ref[tm:tm + SUBLANES, :]
    z_ref[SUBLANES:, :] = _dot_nt(u_ref[...], w_ref[...])


def _proj_conv(u, w_t, conv_w, *, row0, n_norm_cols, seq_len):
    M, D = u.shape
    K, N = conv_w.shape
    tm = _tile(seq_len, 1024, CONV_ROWS)
    tn = _tile(N, 512)
    nj = N // tn
    steps = (M // tm) * nj
    assert row0 % tn == 0 and n_norm_cols % tn == 0
    off = row0 // tn
    cur = lambda s: jnp.minimum(s, steps - 1)
    prev = lambda s: jnp.maximum(s - 1, 0)
    return pl.pallas_call(
        functools.partial(_proj_conv_kernel, nj=nj, tiles_per_seq=seq_len // tm, K=K, n_norm_tiles=n_norm_cols // tn),
        grid=(steps + 1,),
        in_specs=[pl.BlockSpec((tm, D), lambda s: (cur(s) // nj, 0)),
                  pl.BlockSpec((tn, D), lambda s: (cur(s) % nj + off, 0)),
                  pl.BlockSpec((K, tn), lambda s: (0, prev(s) % nj))],
        out_specs=pl.BlockSpec((tm, tn), lambda s: (prev(s) // nj, prev(s) % nj)),
        out_shape=jax.ShapeDtypeStruct((M, N), BF16),
        scratch_shapes=[pltpu.VMEM((tm + SUBLANES, tn), F32), pltpu.VMEM((nj, SUBLANES, tn), F32)],
        compiler_params=_params("arbitrary"),
        name="proj_conv",
    )(u, w_t, conv_w)


def _gates_kernel(u_ref, w_ref, alog_ref, bias_ref, o_ref, *, n_heads):
    x = _dot_nt(w_ref[...], u_ref[...])
    s = x + bias_ref[...]
    softplus = jnp.maximum(s, 0.0) + jnp.log1p(jnp.exp(-jnp.abs(s)))
    g = -jnp.exp(alog_ref[...]) * softplus
    row = lax.broadcasted_iota(jnp.int32, x.shape, 0)
    o_ref[...] = jnp.where(row < n_heads, g, jax.nn.sigmoid(x))


def _gates(u, w_ab_t, a_log, dt_bias):
    M, D = u.shape
    H = a_log.shape[0]
    tm = _tile(M, 512)
    pad = jnp.zeros((H,), F32)
    col = lambda v: jnp.concatenate([v.astype(F32), pad]).reshape(2 * H, 1)
    return pl.pallas_call(
        functools.partial(_gates_kernel, n_heads=H),
        grid=(M // tm,),
        in_specs=[pl.BlockSpec((tm, D), lambda i: (i, 0)),
                  pl.BlockSpec((2 * H, D), lambda i: (0, 0)),
                  pl.BlockSpec((2 * H, 1), lambda i: (0, 0)),
                  pl.BlockSpec((2 * H, 1), lambda i: (0, 0))],
        out_specs=pl.BlockSpec((2 * H, tm), lambda i: (0, i)),
        out_shape=jax.ShapeDtypeStruct((2 * H, M), F32),
        compiler_params=_params("parallel"),
        name="gates",
    )(u, w_ab_t, col(a_log), col(dt_bias))


A_PITCH = CHUNK + SUBLANES


def _delta_kernel(q_ref, k_ref, v_ref, z_ref, g_ref, b_ref, gain_ref, o_ref,
                  gc_ref, s_ref, am_ref, at_ref, tm_ref, u_ref, wq_ref, kt_ref, a_ref, *, n_heads, n_chunks):
    C = CHUNK
    nc = n_heads * n_chunks
    scale = LANES ** -0.5
    row = lax.broadcasted_iota(jnp.int32, (C, C), 0)
    col = lax.broadcasted_iota(jnp.int32, (C, C), 1)
    eye = row == col
    causal = row >= col
    strict = row > col
    zeros_cc = jnp.zeros((C, C), F32)

    @pl.when(pl.program_id(2) == 0)
    def _():
        s_ref[...] = jnp.zeros_like(s_ref)

    prefix = jnp.where(row <= col, 1.0, 0.0)
    for hh in range(n_heads):
        gc_ref[hh] = _dot(g_ref[hh, 0], prefix, precision=HIGHEST)

    def to_col(r):
        return jnp.sum(jnp.where(eye, jnp.broadcast_to(r, (C, C)), 0.0), axis=1, keepdims=True)

    def chunk_slices(hh, c):
        return pl.ds(pl.multiple_of(c * C, C), C), slice(hh * LANES, (hh + 1) * LANES)

    def am_rows(hh, c):
        return pl.ds(pl.multiple_of((hh * n_chunks + c) * A_PITCH, SUBLANES), C)

    def intra(c, carry):
        for hh in range(n_heads):
            rows, lanes = chunk_slices(hh, c)
            k16 = k_ref[rows, lanes]
            k = k16.astype(F32)
            q = q_ref[rows, lanes].astype(F32) * scale
            g_row = gc_ref[hh, pl.ds(c, 1), :]
            g_col = to_col(g_row)
            b_col = to_col(b_ref[hh, 0, pl.ds(c, 1), :])
            decay = jnp.where(causal, jnp.exp(jnp.where(causal, g_col - g_row, 0.0)), 0.0)
            prod = _dot_nt(jnp.concatenate([k * b_col, q], axis=0).astype(BF16), k16)
            a = jnp.where(strict, prod[:C] * decay, 0.0)
            am_ref[am_rows(hh, c), :] = jnp.concatenate([a, zeros_cc], axis=1)
            a_ref[hh, rows, :] = (prod[C:] * decay).astype(BF16)
        return carry

    lax.fori_loop(0, n_chunks, intra, 0)

    for i in range(C):
        at_ref[i] = am_ref[pl.ds(i, nc, stride=A_PITCH), :].T[:C, :]

    zero_blk = jnp.zeros((SUBLANES, nc), F32)
    for i in range(C):
        nb = -(-i // SUBLANES)
        acc = [-at_ref[i, b * SUBLANES:(b + 1) * SUBLANES, :] for b in range(nb)]
        for j in range(1, i):
            a_ij = jnp.broadcast_to(at_ref[i, j:j + 1, :], (SUBLANES, nc))
            for b in range(-(-j // SUBLANES)):
                acc[b] = acc[b] - a_ij * tm_ref[j, b * SUBLANES:(b + 1) * SUBLANES, :]
        for b in range(C // SUBLANES):
            tm_ref[i, b * SUBLANES:(b + 1) * SUBLANES, :] = acc[b] if b < nb else zero_blk

    zeros_pad = jnp.zeros((LANES - C, nc), F32)
    for i in range(C):
        am_ref[pl.ds(i, nc, stride=A_PITCH), :] = jnp.concatenate([tm_ref[i], zeros_pad], axis=0).T

    def solve(c, carry):
        for hh in range(n_heads):
            rows, lanes = chunk_slices(hh, c)
            k = k_ref[rows, lanes].astype(F32)
            v = v_ref[rows, lanes].astype(F32)
            q = q_ref[rows, lanes].astype(F32) * scale
            g_row = gc_ref[hh, pl.ds(c, 1), :]
            g_col = to_col(g_row)
            b_col = to_col(b_ref[hh, 0, pl.ds(c, 1), :])
            e_col = jnp.exp(g_col)
            rhs = jnp.concatenate([v * b_col, k * (b_col * e_col)], axis=1)
            tm = am_ref[am_rows(hh, c), :][:, :C]
            sol = rhs + _dot(tm.astype(BF16), rhs.astype(BF16))
            u_ref[hh, rows, :] = sol[:, :LANES]
            wq_ref[hh, pl.ds(pl.multiple_of(c * 2 * C, 2 * C), C), :] = sol[:, LANES:].astype(BF16)
            wq_ref[hh, pl.ds(pl.multiple_of(c * 2 * C + C, C), C), :] = (q * e_col).astype(BF16)
            k_dec = k * jnp.exp(g_row[:, C - 1:C] - g_col)
            kt_ref[hh, pl.ds(pl.multiple_of(c * LANES, LANES), LANES), :] = k_dec.T.astype(BF16)
        return carry

    lax.fori_loop(0, n_chunks, solve, 0)

    def recur(c, carry):
        heads = range(n_heads)
        rows = pl.ds(pl.multiple_of(c * C, C), C)
        S = [s_ref[hh] for hh in heads]
        r = [_dot(wq_ref[hh, pl.ds(pl.multiple_of(c * 2 * C, 2 * C), 2 * C), :], S[hh].astype(BF16))
             for hh in heads]
        v16 = [(u_ref[hh, rows, :] - r[hh][:C]).astype(BF16) for hh in heads]
        for hh in heads:
            g_last = gc_ref[hh, pl.ds(c, 1), :][:, C - 1:C]
            kt = kt_ref[hh, pl.ds(pl.multiple_of(c * LANES, LANES), LANES), :]
            s_ref[hh] = S[hh] * jnp.exp(g_last) + _dot(kt, v16[hh])
        for hh in heads:
            lanes = slice(hh * LANES, (hh + 1) * LANES)
            o = r[hh][C:] + _dot(a_ref[hh, rows, :], v16[hh])
            o_ref[rows, lanes] = (_rms(o, gain_ref[...]) * z_ref[rows, lanes].astype(F32)).astype(o_ref.dtype)
        return carry

    lax.fori_loop(0, n_chunks, recur, 0)


def _delta(qkvz, gb, gain, *, n_heads, seq_len):
    M = qkvz.shape[0]
    W = n_heads * LANES
    B = M // seq_len
    nh = min(n_heads, 8)
    ncb = LANES // nh
    tb = ncb * CHUNK
    hw = nh * LANES
    nb = W // hw
    nt = seq_len // tb
    assert LANES % nh == 0 and n_heads % nh == 0 and seq_len % tb == 0 and ncb % SUBLANES == 0
    tok = lambda off: pl.BlockSpec((tb, hw), lambda b, h, t: (b * nt + t, h + off))
    return pl.pallas_call(
        functools.partial(_delta_kernel, n_heads=nh, n_chunks=ncb),
        grid=(B, nb, nt),
        in_specs=[tok(0), tok(nb), tok(2 * nb), tok(3 * nb),
                  pl.BlockSpec((nh, 1, ncb, CHUNK), lambda b, h, t: (h, b, t, 0)),
                  pl.BlockSpec((nh, 1, ncb, CHUNK), lambda b, h, t: (h + nb, b, t, 0)),
                  pl.BlockSpec((1, LANES), lambda b, h, t: (0, 0))],
        out_specs=tok(0),
        out_shape=jax.ShapeDtypeStruct((M, W), BF16),
        scratch_shapes=[pltpu.VMEM((nh, ncb, CHUNK), F32),
                        pltpu.VMEM((nh, LANES, LANES), F32),
                        pltpu.VMEM((LANES * A_PITCH, LANES), F32),
                        pltpu.VMEM((CHUNK, CHUNK, LANES), F32),
                        pltpu.VMEM((CHUNK, CHUNK, LANES), F32),
                        pltpu.VMEM((nh, tb, LANES), F32),
                        pltpu.VMEM((nh, 2 * tb, LANES), BF16),
                        pltpu.VMEM((nh, ncb * LANES, CHUNK), BF16),
                        pltpu.VMEM((nh, tb, CHUNK), BF16)],
        compiler_params=_params("parallel", "parallel", "arbitrary"),
        name="delta_rule",
    )(qkvz, qkvz, qkvz, qkvz, gb, gb, gain.reshape(1, LANES))


def _merge_kernel(u_ref, yc_ref, yd_ref, h_ref, wgc_ref, wgd_ref, wc_ref, wdn_ref, wo_ref, o_ref):
    @pl.when(pl.program_id(1) == 0)
    def _():
        def init(rows):
            o_ref[rows, :] = h_ref[rows, :]
        _for_rows(h_ref.shape[0], init)

    u = u_ref[...]
    gc = jax.nn.sigmoid(_dot_nt(u, wgc_ref[...]))
    gd = jax.nn.sigmoid(_dot_nt(u, wgd_ref[...]))
    merged = gc * _dot(yc_ref[...], wc_ref[...]) + gd * _dot(yd_ref[...], wdn_ref[...])
    o_ref[...] += _dot(merged.astype(BF16), wo_ref[...])


def _merge(u, y_conv, y_dn, h, wg_t, wc, wdn, wo):
    M, D = h.shape
    Wc = y_conv.shape[1]
    Wd = y_dn.shape[1]
    tm = _tile(M, 512, ROW_CHUNK)
    tn = _tile(D, 256)
    nn = D // tn
    row = lambda i, j: (i, 0)
    colw = lambda i, j: (0, j)
    return pl.pallas_call(
        _merge_kernel,
        grid=(M // tm, nn),
        in_specs=[pl.BlockSpec((tm, D), row), pl.BlockSpec((tm, Wc), row), pl.BlockSpec((tm, Wd), row),
                  _single((tm, D), row),
                  pl.BlockSpec((tn, D), lambda i, j: (j, 0)), pl.BlockSpec((tn, D), lambda i, j: (j + nn, 0)),
                  pl.BlockSpec((Wc, tn), colw), pl.BlockSpec((Wd, tn), colw),
                  pl.BlockSpec((tn, D), lambda i, j: (j, 0))],
        out_specs=_single((tm, D), row),
        out_shape=jax.ShapeDtypeStruct((M, D), F32),
        compiler_params=_params("parallel", "arbitrary"),
        name="merge_out",
    )(u, y_conv, y_dn, h, wg_t, wg_t, wc, wdn, wo)


def kernel(x, ffn1_norm, ffn1_w_gate, ffn1_w_up, ffn1_w_down, mix_norm, w_in, conv_mixer_w, dn_conv_w, dn_a_log, dn_dt_bias, dn_out_norm, w_conv_branch, w_dn_branch, w_out, ffn2_norm, ffn2_w_gate, ffn2_w_up, ffn2_w_down, final_norm):
    B, T, D = x.shape
    M = B * T
    depth = ffn1_norm.shape[0]
    Wc = conv_mixer_w.shape[1]
    Wd = dn_conv_w.shape[1] // 3
    H = dn_a_log.shape[1]
    assert Wd == H * LANES and dn_out_norm.shape[1] == LANES and T % CHUNK == 0
    c_q = 3 * Wc
    c_z = c_q + 3 * Wd
    c_a = c_z + Wd
    c_gc = c_a + 2 * H
    c_gd = c_gc + D
    assert w_in.shape[2] == c_gd + D
    bf = lambda w: w.astype(BF16)

    h = x.reshape(M, D)
    for l in range(depth):
        last = l == depth - 1
        casts = (w_in[l].T, w_conv_branch[l], w_dn_branch[l], w_out[l], ffn2_w_gate[l], ffn2_w_up[l], ffn2_w_down[l])
        h, u, w_in16, wc16, wdn16, wo16, wg2, wu2, wd2 = _ffn(
            h, ffn1_norm[l], bf(ffn1_w_gate[l]), bf(ffn1_w_up[l]), bf(ffn1_w_down[l]), mix_norm[l],
            final=False, casts=casts)

        gb = _gates(u, w_in16[c_a:c_gc], dn_a_log[l], dn_dt_bias[l])
        y_conv = _proj_gate(u, w_in16, conv_mixer_w[l].T, width=Wc, seq_len=T)
        taps = dn_conv_w.shape[2]
        identity = jnp.zeros((taps, Wd), F32).at[taps - 1].set(1.0)
        qkvz = _proj_conv(u, w_in16, jnp.concatenate([dn_conv_w[l].T, identity], axis=1),
                          row0=c_q, n_norm_cols=2 * Wd, seq_len=T)
        y_dn = _delta(qkvz, gb.reshape(2 * H, B, T // CHUNK, CHUNK), dn_out_norm[l], n_heads=H, seq_len=T)
        h = _merge(u, y_conv, y_dn, h, w_in16[c_gc:], wc16, wdn16, wo16)

        if last:
            h = _ffn(h, ffn2_norm[l], wg2, wu2, wd2, final_norm, final=True)
        else:
            h = _ffn(h, ffn2_norm[l], wg2, wu2, wd2, final_norm, final=False)[0]
    return h.reshape(B, T, D)
```

```python
import functools

import jax
import jax.numpy as jnp
from jax import lax
from jax.experimental import pallas as pl
from jax.experimental.pallas import tpu as pltpu

F32 = jnp.float32
BF16 = jnp.bfloat16
EPS = 1e-6
L2_EPS = 1e-6
CHUNK = 64
LANES = 128
SUBLANES = 8
MXU_COLS = 256
ROW_CHUNK = 32
FFN_TILES = (512, 256)
V7X_VMEM_LIMIT_BYTES = 56 * 1024 * 1024
HIGHEST = lax.Precision.HIGHEST


def _params(*sem):
    return pltpu.CompilerParams(dimension_semantics=sem, vmem_limit_bytes=V7X_VMEM_LIMIT_BYTES)


def _tile(n, pref, mult=LANES):
    if n <= pref:
        return n
    t = (pref // mult) * mult
    while t > 0 and n % t:
        t -= mult
    assert t > 0, (n, pref, mult)
    return t


def _single(block, index_map):
    return pl.BlockSpec(block, index_map, pipeline_mode=pl.Buffered(1))


def _rms(h, gain):
    return h * lax.rsqrt(jnp.mean(h * h, axis=-1, keepdims=True) + EPS) * gain


def _for_rows(n_rows, fn):
    def body(r, carry):
        fn(pl.ds(pl.multiple_of(r * ROW_CHUNK, ROW_CHUNK), ROW_CHUNK))
        return carry
    lax.fori_loop(0, n_rows // ROW_CHUNK, body, 0)


def _dot(a, b, **kw):
    return jnp.dot(a, b, preferred_element_type=F32, **kw)


def _dot_nt(a, b, **kw):
    return lax.dot_general(a, b, (((1,), (1,)), ((), ())), preferred_element_type=F32, **kw)


def _dot_tn(a, b, **kw):
    return lax.dot_general(a, b, (((0,), (0,)), ((), ())), preferred_element_type=F32, **kw)


def _spread(shape, grid):
    R, Cn = shape
    gi, gj = (1,) * (2 - len(grid)) + tuple(grid)
    best = None
    for shift in range(12):
        bc = -(-Cn // (LANES << shift)) * LANES
        ncol = -(-Cn // bc)
        if ncol != 1 << shift:
            continue
        nr_max = (gi * gj) // ncol
        if nr_max == 0:
            break
        br = -(-R // (nr_max * 2 * SUBLANES)) * 2 * SUBLANES
        nr = -(-R // br)
        key = (R % br == 0 and Cn % bc == 0 and 2 * nr * ncol >= gi * gj, nr * ncol)
        if best is None or key > best[0]:
            best = (key, br, bc, nr, shift)
    _, br, bc, nr, shift = best

    def index_map(*ids):
        step = ids[0] if len(ids) == 1 else ids[0] * gj + ids[1]
        s = jnp.minimum(step, (nr << shift) - 1)
        return s >> shift, s & ((1 << shift) - 1)
    return (br, bc), index_map


def _cast_specs(casts, grid):
    specs = [pl.BlockSpec(*_spread(a.shape, grid)) for a in casts]
    return specs, [jax.ShapeDtypeStruct(a.shape, BF16) for a in casts]


def _do_casts(src_refs, dst_refs):
    for src, dst in zip(src_refs, dst_refs):
        dst[...] = src[...].astype(BF16)


def _ffn_kernel(h_ref, g_ref, wg_ref, wu_ref, wd_ref, ng_ref, *refs, final, n_casts, last_cols):
    cast_in, refs = refs[:n_casts], refs[n_casts:]
    if final:
        o_ref, xn_ref = refs
    else:
        o_ref, u_ref, *cast_out, xn_ref = refs
        _do_casts(cast_in, cast_out)
    j = pl.program_id(1)

    @pl.when(j == 0)
    def _():
        def init(rows):
            h = h_ref[rows, :]
            xn_ref[rows, :] = _rms(h, g_ref[...]).astype(BF16)
            o_ref[rows, :] = h
        _for_rows(h_ref.shape[0], init)

    def ff_tile(n_cols):
        xn = xn_ref[...]
        parts = [(c, min(c + MXU_COLS, n_cols)) for c in range(0, n_cols, MXU_COLS)]
        gate_up = [(_dot(xn, wg_ref[:, a:b]), _dot(xn, wu_ref[:, a:b])) for a, b in parts]
        for (a, b), (gate, up) in zip(parts, gate_up):
            act = (gate * jax.nn.sigmoid(gate) * (0.5 * up)).astype(BF16)
            o_ref[...] += _dot(act, wd_ref[a:b, :])

    last = pl.num_programs(1) - 1
    if last_cols == wg_ref.shape[1]:
        ff_tile(last_cols)
    else:
        pl.when(j < last)(lambda: ff_tile(wg_ref.shape[1]))
        pl.when(j == last)(lambda: ff_tile(last_cols))

    @pl.when(j == last)
    def _():
        def fin(rows):
            y = _rms(o_ref[rows, :], ng_ref[...])
            if final:
                o_ref[rows, :] = y
            else:
                u_ref[rows, :] = y.astype(BF16)
        _for_rows(h_ref.shape[0], fin)


def _ffn(h, gain, wg, wu, wd, next_gain, *, final, casts=()):
    M, D = h.shape
    F = wg.shape[1]
    tm = _tile(M, 512, ROW_CHUNK)

    def vmem_bytes(tf):
        steps = (M // tm) * -(-F // tf)
        rows = tm * D * (4 + 4 + 2 + (0 if final else 2))
        side = 0 if final else sum(2 * 6 * -(-a.size // steps) for a in casts)
        return rows + 2 * 3 * D * tf * 2 + side + 3 * tm * tf * 4

    tf = next((t for t in FFN_TILES if vmem_bytes(min(F, t)) <= V7X_VMEM_LIMIT_BYTES), FFN_TILES[-1])
    tf = min(F, tf)
    nj = -(-F // tf)
    last_cols = F - (nj - 1) * tf
    assert last_cols % LANES == 0
    grid = (M // tm, nj)
    row = lambda i, j: (i, 0)
    out_shape = [jax.ShapeDtypeStruct((M, D), F32)]
    out_specs = [_single((tm, D), row)]
    cast_specs = []
    if not final:
        out_shape.append(jax.ShapeDtypeStruct((M, D), BF16))
        out_specs.append(_single((tm, D), row))
        cast_specs, cast_shapes = _cast_specs(casts, grid)
        out_specs += cast_specs
        out_shape += cast_shapes
    res = pl.pallas_call(
        functools.partial(_ffn_kernel, final=final, n_casts=len(cast_specs), last_cols=last_cols),
        grid=grid,
        in_specs=[
            _single((tm, D), row),
            pl.BlockSpec((1, D), lambda i, j: (0, 0)),
            pl.BlockSpec((D, tf), lambda i, j: (0, j)),
            pl.BlockSpec((D, tf), lambda i, j: (0, j)),
            pl.BlockSpec((tf, D), lambda i, j: (j, 0)),
            pl.BlockSpec((1, D), lambda i, j: (0, 0)),
        ] + cast_specs,
        out_specs=out_specs,
        out_shape=out_shape,
        scratch_shapes=[pltpu.VMEM((tm, D), BF16)],
        compiler_params=_params("parallel", "arbitrary"),
        name="ffn_final" if final else "ffn",
    )(h, gain.reshape(1, D), wg, wu, wd, next_gain.reshape(1, D), *casts)
    return res[0] if final else res


CONV_ROWS = 64


def _conv_piece(z_ref, w_ref, K, r0, lanes):
    zz = z_ref[pl.ds(r0, CONV_ROWS + SUBLANES), lanes]
    acc = zz[SUBLANES:, :] * w_ref[K - 1:K, lanes]
    for s in range(1, K):
        acc = acc + pltpu.roll(zz, s, axis=0)[SUBLANES:, :] * w_ref[K - 1 - s:K - s, lanes]
    return acc


def _load_halo(z_ref, halo_ref, j, first):
    @pl.when(first)
    def _():
        z_ref[0:SUBLANES, :] = jnp.zeros((SUBLANES, z_ref.shape[1]), F32)

    @pl.when(jnp.logical_not(first))
    def _():
        z_ref[0:SUBLANES, :] = halo_ref[j]


def _proj_gate_kernel(u_ref, wb_ref, wc_ref, wx_ref, cw_ref, *refs, tiles_per_seq, K, n_casts):
    cast_in, o_ref, cast_out, (z_ref, halo_ref) = refs[:n_casts], refs[n_casts], refs[n_casts + 1:-2], refs[-2:]
    _do_casts(cast_in, cast_out)
    i, j = pl.program_id(0), pl.program_id(1)
    tm, tn = o_ref.shape
    _load_halo(z_ref, halo_ref, j, (i % tiles_per_seq) == 0)
    u = u_ref[...]
    z_ref[SUBLANES:, :] = _dot_nt(u, wc_ref[...]) * _dot_nt(u, wx_ref[...])
    halo_ref[j] = z_ref[tm:tm + SUBLANES, :]
    b = _dot_nt(u, wb_ref[...])
    for r0 in range(0, tm, CONV_ROWS):
        for c0 in range(0, tn, LANES):
            lanes = slice(c0, c0 + LANES)
            y = b[r0:r0 + CONV_ROWS, lanes] * _conv_piece(z_ref, cw_ref, K, r0, lanes)
            o_ref[r0:r0 + CONV_ROWS, lanes] = y.astype(o_ref.dtype)


def _proj_gate(u, w_t, conv_w, *, width, seq_len, casts=()):
    M, D = u.shape
    K = conv_w.shape[0]
    tm = _tile(seq_len, 1024, CONV_ROWS)
    tn = _tile(width, MXU_COLS)
    nj = width // tn
    grid = (M // tm, nj)
    wspec = lambda off: pl.BlockSpec((tn, D), lambda i, j: (j + off, 0))
    cast_specs, cast_shapes = _cast_specs(casts, grid)
    return pl.pallas_call(
        functools.partial(_proj_gate_kernel, tiles_per_seq=seq_len // tm, K=K, n_casts=len(casts)),
        grid=grid,
        in_specs=[pl.BlockSpec((tm, D), lambda i, j: (i, 0)), wspec(0), wspec(nj), wspec(2 * nj),
                  pl.BlockSpec((K, tn), lambda i, j: (0, j))] + cast_specs,
        out_specs=[pl.BlockSpec((tm, tn), lambda i, j: (i, j))] + cast_specs,
        out_shape=[jax.ShapeDtypeStruct((M, width), BF16)] + cast_shapes,
        scratch_shapes=[pltpu.VMEM((tm + SUBLANES, tn), F32), pltpu.VMEM((nj, SUBLANES, tn), F32)],
        compiler_params=_params("arbitrary", "arbitrary"),
        name="proj_gate",
    )(u, w_t, w_t, w_t, conv_w, *casts)


def _proj_conv_kernel(u_ref, w_ref, cw_ref, *refs, nj, tiles_per_seq, K, n_norm_tiles, n_casts):
    cast_in, o_ref, cast_out, (z_ref, halo_ref) = refs[:n_casts], refs[n_casts], refs[n_casts + 1:-2], refs[-2:]
    _do_casts(cast_in, cast_out)
    s = pl.program_id(0)
    tm, tn = o_ref.shape

    @pl.when(s == 0)
    def _():
        z_ref[...] = jnp.zeros_like(z_ref)

    t = jnp.maximum(s - 1, 0)
    i, j = t // nj, t % nj
    _load_halo(z_ref, halo_ref, j, (i % tiles_per_seq) == 0)
    normed = j < n_norm_tiles
    for r0 in range(0, tm, CONV_ROWS):
        for c0 in range(0, tn, LANES):
            lanes = slice(c0, c0 + LANES)
            acc = _conv_piece(z_ref, cw_ref, K, r0, lanes)
            y = acc * jax.nn.sigmoid(acc)
            ss = jnp.sum(y * y, axis=-1, keepdims=True)
            y = y * jnp.where(normed, lax.rsqrt(ss + L2_EPS), 1.0)
            o_ref[r0:r0 + CONV_ROWS, lanes] = y.astype(o_ref.dtype)
    halo_ref[j] = z_ref[tm:tm + SUBLANES, :]
    z_ref[SUBLANES:, :] = _dot_nt(u_ref[...], w_ref[...])


def _proj_conv(u, w_t, conv_w, *, row0, n_norm_cols, seq_len, casts=()):
    M, D = u.shape
    K, N = conv_w.shape
    tm = _tile(seq_len, 1024, CONV_ROWS)
    tn = _tile(N, 512)
    nj = N // tn
    steps = (M // tm) * nj
    assert row0 % tn == 0 and n_norm_cols % tn == 0
    off = row0 // tn
    cur = lambda s: jnp.minimum(s, steps - 1)
    prev = lambda s: jnp.maximum(s - 1, 0)
    cast_specs, cast_shapes = _cast_specs(casts, (steps + 1,))
    return pl.pallas_call(
        functools.partial(_proj_conv_kernel, nj=nj, tiles_per_seq=seq_len // tm, K=K, n_norm_tiles=n_norm_cols // tn,
                          n_casts=len(casts)),
        grid=(steps + 1,),
        in_specs=[pl.BlockSpec((tm, D), lambda s: (cur(s) // nj, 0)),
                  pl.BlockSpec((tn, D), lambda s: (cur(s) % nj + off, 0)),
                  pl.BlockSpec((K, tn), lambda s: (0, prev(s) % nj))] + cast_specs,
        out_specs=[pl.BlockSpec((tm, tn), lambda s: (prev(s) // nj, prev(s) % nj))] + cast_specs,
        out_shape=[jax.ShapeDtypeStruct((M, N), BF16)] + cast_shapes,
        scratch_shapes=[pltpu.VMEM((tm + SUBLANES, tn), F32), pltpu.VMEM((nj, SUBLANES, tn), F32)],
        compiler_params=_params("arbitrary"),
        name="proj_conv",
    )(u, w_t, conv_w, *casts)


def _gates_kernel(u_ref, w_ref, alog_ref, bias_ref, o_ref, *, n_heads):
    x = _dot_nt(w_ref[...], u_ref[...])
    s = x + bias_ref[...]
    softplus = jnp.maximum(s, 0.0) + jnp.log1p(jnp.exp(-jnp.abs(s)))
    g = -jnp.exp(alog_ref[...]) * softplus
    row = lax.broadcasted_iota(jnp.int32, x.shape, 0)
    o_ref[...] = jnp.where(row < n_heads, g, jax.nn.sigmoid(x))


def _gates(u, w_ab_t, a_log, dt_bias):
    M, D = u.shape
    H = a_log.shape[0]
    tm = _tile(M, 512)
    pad = jnp.zeros((H,), F32)
    col = lambda v: jnp.concatenate([v.astype(F32), pad]).reshape(2 * H, 1)
    return pl.pallas_call(
        functools.partial(_gates_kernel, n_heads=H),
        grid=(M // tm,),
        in_specs=[pl.BlockSpec((tm, D), lambda i: (i, 0)),
                  pl.BlockSpec((2 * H, D), lambda i: (0, 0)),
                  pl.BlockSpec((2 * H, 1), lambda i: (0, 0)),
                  pl.BlockSpec((2 * H, 1), lambda i: (0, 0))],
        out_specs=pl.BlockSpec((2 * H, tm), lambda i: (0, i)),
        out_shape=jax.ShapeDtypeStruct((2 * H, M), F32),
        compiler_params=_params("parallel"),
        name="gates",
    )(u, w_ab_t, col(a_log), col(dt_bias))


A_PITCH = CHUNK + SUBLANES


def _delta_kernel(q_ref, k_ref, v_ref, z_ref, g_ref, b_ref, gain_ref, o_ref,
                  gc_ref, s_ref, am_ref, at_ref, tm_ref, u_ref, wq_ref, kt_ref, a_ref, *, n_heads, n_chunks):
    C = CHUNK
    nc = n_heads * n_chunks
    scale = LANES ** -0.5
    row = lax.broadcasted_iota(jnp.int32, (C, C), 0)
    col = lax.broadcasted_iota(jnp.int32, (C, C), 1)
    eye = row == col
    causal = row >= col
    strict = row > col
    zeros_cc = jnp.zeros((C, C), F32)

    @pl.when(pl.program_id(2) == 0)
    def _():
        s_ref[...] = jnp.zeros_like(s_ref)

    prefix = jnp.where(row <= col, 1.0, 0.0)
    for hh in range(n_heads):
        gc_ref[hh] = _dot(g_ref[hh, 0], prefix, precision=HIGHEST)

    def to_col(r):
        return jnp.sum(jnp.where(eye, jnp.broadcast_to(r, (C, C)), 0.0), axis=1, keepdims=True)

    def chunk_slices(hh, c):
        return pl.ds(pl.multiple_of(c * C, C), C), slice(hh * LANES, (hh + 1) * LANES)

    def am_rows(hh, c):
        return pl.ds(pl.multiple_of((hh * n_chunks + c) * A_PITCH, SUBLANES), C)

    def intra(c, carry):
        for hh in range(n_heads):
            rows, lanes = chunk_slices(hh, c)
            k16 = k_ref[rows, lanes]
            k = k16.astype(F32)
            q = q_ref[rows, lanes].astype(F32) * scale
            g_row = gc_ref[hh, pl.ds(c, 1), :]
            g_col = to_col(g_row)
            b_col = to_col(b_ref[hh, 0, pl.ds(c, 1), :])
            decay = jnp.where(causal, jnp.exp(jnp.where(causal, g_col - g_row, 0.0)), 0.0)
            prod = _dot_nt(jnp.concatenate([k * b_col, q], axis=0).astype(BF16), k16)
            a = jnp.where(strict, prod[:C] * decay, 0.0)
            am_ref[am_rows(hh, c), :] = jnp.concatenate([a, zeros_cc], axis=1)
            a_ref[hh, rows, :] = (prod[C:] * decay).astype(BF16)
        return carry

    lax.fori_loop(0, n_chunks, intra, 0)

    for i in range(C):
        at_ref[i] = am_ref[pl.ds(i, nc, stride=A_PITCH), :].T[:C, :]

    zero_blk = jnp.zeros((SUBLANES, nc), F32)
    for i in range(C):
        nb = -(-i // SUBLANES)
        acc = [-at_ref[i, b * SUBLANES:(b + 1) * SUBLANES, :] for b in range(nb)]
        for j in range(1, i):
            a_ij = jnp.broadcast_to(at_ref[i, j:j + 1, :], (SUBLANES, nc))
            for b in range(-(-j // SUBLANES)):
                acc[b] = acc[b] - a_ij * tm_ref[j, b * SUBLANES:(b + 1) * SUBLANES, :]
        for b in range(C // SUBLANES):
            tm_ref[i, b * SUBLANES:(b + 1) * SUBLANES, :] = acc[b] if b < nb else zero_blk

    zeros_pad = jnp.zeros((LANES - C, nc), F32)
    for i in range(C):
        am_ref[pl.ds(i, nc, stride=A_PITCH), :] = jnp.concatenate([tm_ref[i], zeros_pad], axis=0).T

    def solve(c, carry):
        for hh in range(n_heads):
            rows, lanes = chunk_slices(hh, c)
            k = k_ref[rows, lanes].astype(F32)
            v = v_ref[rows, lanes].astype(F32)
            q = q_ref[rows, lanes].astype(F32) * scale
            g_row = gc_ref[hh, pl.ds(c, 1), :]
            g_col = to_col(g_row)
            b_col = to_col(b_ref[hh, 0, pl.ds(c, 1), :])
            e_col = jnp.exp(g_col)
            rhs = jnp.concatenate([v * b_col, k * (b_col * e_col)], axis=1)
            tm = am_ref[am_rows(hh, c), :][:, :C]
            sol = rhs + _dot(tm.astype(BF16), rhs.astype(BF16))
            u_ref[hh, rows, :] = sol[:, :LANES]
            wq_ref[hh, pl.ds(pl.multiple_of(c * 2 * C, 2 * C), C), :] = sol[:, LANES:].astype(BF16)
            wq_ref[hh, pl.ds(pl.multiple_of(c * 2 * C + C, C), C), :] = (q * e_col).astype(BF16)
            k_dec = k * jnp.exp(g_row[:, C - 1:C] - g_col)
            kt_ref[hh, pl.ds(pl.multiple_of(c * LANES, LANES), LANES), :] = k_dec.T.astype(BF16)
        return carry

    lax.fori_loop(0, n_chunks, solve, 0)

    def recur(c, carry):
        heads = range(n_heads)
        rows = pl.ds(pl.multiple_of(c * C, C), C)
        S = [s_ref[hh] for hh in heads]
        r = [_dot(wq_ref[hh, pl.ds(pl.multiple_of(c * 2 * C, 2 * C), 2 * C), :], S[hh].astype(BF16))
             for hh in heads]
        v16 = [(u_ref[hh, rows, :] - r[hh][:C]).astype(BF16) for hh in heads]
        for hh in heads:
            g_last = gc_ref[hh, pl.ds(c, 1), :][:, C - 1:C]
            kt = kt_ref[hh, pl.ds(pl.multiple_of(c * LANES, LANES), LANES), :]
            s_ref[hh] = S[hh] * jnp.exp(g_last) + _dot(kt, v16[hh])
        for hh in heads:
            lanes = slice(hh * LANES, (hh + 1) * LANES)
            o = r[hh][C:] + _dot(a_ref[hh, rows, :], v16[hh])
            o_ref[rows, lanes] = (_rms(o, gain_ref[...]) * z_ref[rows, lanes].astype(F32)).astype(o_ref.dtype)
        return carry

    lax.fori_loop(0, n_chunks, recur, 0)


def _delta(qkvz, gb, gain, *, n_heads, seq_len):
    M = qkvz.shape[0]
    W = n_heads * LANES
    B = M // seq_len
    nh = min(n_heads, 8)
    ncb = LANES // nh
    tb = ncb * CHUNK
    hw = nh * LANES
    nb = W // hw
    nt = seq_len // tb
    assert LANES % nh == 0 and n_heads % nh == 0 and seq_len % tb == 0 and ncb % SUBLANES == 0
    tok = lambda off: pl.BlockSpec((tb, hw), lambda b, h, t: (b * nt + t, h + off))
    return pl.pallas_call(
        functools.partial(_delta_kernel, n_heads=nh, n_chunks=ncb),
        grid=(B, nb, nt),
        in_specs=[tok(0), tok(nb), tok(2 * nb), tok(3 * nb),
                  pl.BlockSpec((nh, 1, ncb, CHUNK), lambda b, h, t: (h, b, t, 0)),
                  pl.BlockSpec((nh, 1, ncb, CHUNK), lambda b, h, t: (h + nb, b, t, 0)),
                  pl.BlockSpec((1, LANES), lambda b, h, t: (0, 0))],
        out_specs=tok(0),
        out_shape=jax.ShapeDtypeStruct((M, W), BF16),
        scratch_shapes=[pltpu.VMEM((nh, ncb, CHUNK), F32),
                        pltpu.VMEM((nh, LANES, LANES), F32),
                        pltpu.VMEM((LANES * A_PITCH, LANES), F32),
                        pltpu.VMEM((CHUNK, CHUNK, LANES), F32),
                        pltpu.VMEM((CHUNK, CHUNK, LANES), F32),
                        pltpu.VMEM((nh, tb, LANES), F32),
                        pltpu.VMEM((nh, 2 * tb, LANES), BF16),
                        pltpu.VMEM((nh, ncb * LANES, CHUNK), BF16),
                        pltpu.VMEM((nh, tb, CHUNK), BF16)],
        compiler_params=_params("parallel", "parallel", "arbitrary"),
        name="delta_rule",
    )(qkvz, qkvz, qkvz, qkvz, gb, gb, gain.reshape(1, LANES))


def _merge_kernel(u_ref, yc_ref, yd_ref, h_ref, wgc_ref, wgd_ref, wc_ref, wdn_ref, wo_ref, o_ref):
    @pl.when(pl.program_id(1) == 0)
    def _():
        def init(rows):
            o_ref[rows, :] = h_ref[rows, :]
        _for_rows(h_ref.shape[0], init)

    u = u_ref[...]
    gc = jax.nn.sigmoid(_dot_nt(u, wgc_ref[...]))
    gd = jax.nn.sigmoid(_dot_nt(u, wgd_ref[...]))
    merged = gc * _dot(yc_ref[...], wc_ref[...]) + gd * _dot(yd_ref[...], wdn_ref[...])
    o_ref[...] += _dot(merged.astype(BF16), wo_ref[...])


def _merge(u, y_conv, y_dn, h, wg_t, wc, wdn, wo):
    M, D = h.shape
    Wc = y_conv.shape[1]
    Wd = y_dn.shape[1]
    tm = _tile(M, 512, ROW_CHUNK)
    tn = _tile(D, 256)
    nn = D // tn
    row = lambda i, j: (i, 0)
    colw = lambda i, j: (0, j)
    return pl.pallas_call(
        _merge_kernel,
        grid=(M // tm, nn),
        in_specs=[pl.BlockSpec((tm, D), row), pl.BlockSpec((tm, Wc), row), pl.BlockSpec((tm, Wd), row),
                  _single((tm, D), row),
                  pl.BlockSpec((tn, D), lambda i, j: (j, 0)), pl.BlockSpec((tn, D), lambda i, j: (j + nn, 0)),
                  pl.BlockSpec((Wc, tn), colw), pl.BlockSpec((Wd, tn), colw),
                  pl.BlockSpec((tn, D), lambda i, j: (j, 0))],
        out_specs=_single((tm, D), row),
        out_shape=jax.ShapeDtypeStruct((M, D), F32),
        compiler_params=_params("parallel", "arbitrary"),
        name="merge_out",
    )(u, y_conv, y_dn, h, wg_t, wg_t, wc, wdn, wo)


def kernel(x, ffn1_norm, ffn1_w_gate, ffn1_w_up, ffn1_w_down, mix_norm, w_in, conv_mixer_w, dn_conv_w, dn_a_log, dn_dt_bias, dn_out_norm, w_conv_branch, w_dn_branch, w_out, ffn2_norm, ffn2_w_gate, ffn2_w_up, ffn2_w_down, final_norm):
    B, T, D = x.shape
    M = B * T
    depth = ffn1_norm.shape[0]
    Wc = conv_mixer_w.shape[1]
    Wd = dn_conv_w.shape[1] // 3
    H = dn_a_log.shape[1]
    assert Wd == H * LANES and dn_out_norm.shape[1] == LANES and T % CHUNK == 0
    c_q = 3 * Wc
    c_z = c_q + 3 * Wd
    c_a = c_z + Wd
    c_gc = c_a + 2 * H
    c_gd = c_gc + D
    assert w_in.shape[2] == c_gd + D
    bf = lambda w: w.astype(BF16)

    h = x.reshape(M, D)
    for l in range(depth):
        last = l == depth - 1
        h, u, w_in16 = _ffn(h, ffn1_norm[l], bf(ffn1_w_gate[l]), bf(ffn1_w_up[l]), bf(ffn1_w_down[l]), mix_norm[l],
                            final=False, casts=(w_in[l].T,))

        gb = _gates(u, w_in16[c_a:c_gc], dn_a_log[l], dn_dt_bias[l])
        y_conv, wc16, wdn16, wo16 = _proj_gate(u, w_in16, conv_mixer_w[l].T, width=Wc, seq_len=T,
                                               casts=(w_conv_branch[l], w_dn_branch[l], w_out[l]))
        taps = dn_conv_w.shape[2]
        identity = jnp.zeros((taps, Wd), F32).at[taps - 1].set(1.0)
        qkvz, wg2, wu2, wd2 = _proj_conv(u, w_in16, jnp.concatenate([dn_conv_w[l].T, identity], axis=1),
                                         row0=c_q, n_norm_cols=2 * Wd, seq_len=T,
                                         casts=(ffn2_w_gate[l], ffn2_w_up[l], ffn2_w_down[l]))
        y_dn = _delta(qkvz, gb.reshape(2 * H, B, T // CHUNK, CHUNK), dn_out_norm[l], n_heads=H, seq_len=T)
        h = _merge(u, y_conv, y_dn, h, w_in16[c_gc:], wc16, wdn16, wo16)

        if last:
            h = _ffn(h, ffn2_norm[l], wg2, wu2, wd2, final_norm, final=True)
        else:
            h = _ffn(h, ffn2_norm[l], wg2, wu2, wd2, final_norm, final=False)[0]
    return h.reshape(B, T, D)
```

```python
import functools

import jax
import jax.numpy as jnp
from jax import lax
from jax.experimental import pallas as pl
from jax.experimental.pallas import tpu as pltpu

F32 = jnp.float32
BF16 = jnp.bfloat16
EPS = 1e-6
L2_EPS = 1e-6
CHUNK = 64
LANES = 128
SUBLANES = 8
MXU_COLS = 256
ROW_CHUNK = 32
FFN_ROWS = 512
FFN_TILES = (512, 256)
V7X_VMEM_LIMIT_BYTES = 56 * 1024 * 1024
HIGHEST = lax.Precision.HIGHEST


def _params(*sem):
    return pltpu.CompilerParams(dimension_semantics=sem, vmem_limit_bytes=V7X_VMEM_LIMIT_BYTES)


def _tile(n, pref, mult=LANES):
    if n <= pref:
        return n
    t = (pref // mult) * mult
    while t > 0 and n % t:
        t -= mult
    assert t > 0, (n, pref, mult)
    return t


def _single(block, index_map):
    return pl.BlockSpec(block, index_map, pipeline_mode=pl.Buffered(1))


def _rms(h, gain):
    return h * lax.rsqrt(jnp.mean(h * h, axis=-1, keepdims=True) + EPS) * gain


def _for_rows(n_rows, fn):
    def body(r, carry):
        fn(pl.ds(pl.multiple_of(r * ROW_CHUNK, ROW_CHUNK), ROW_CHUNK))
        return carry
    lax.fori_loop(0, n_rows // ROW_CHUNK, body, 0)


def _dot(a, b, **kw):
    return jnp.dot(a, b, preferred_element_type=F32, **kw)


def _dot_nt(a, b, **kw):
    return lax.dot_general(a, b, (((1,), (1,)), ((), ())), preferred_element_type=F32, **kw)


def _dot_tn(a, b, **kw):
    return lax.dot_general(a, b, (((0,), (0,)), ((), ())), preferred_element_type=F32, **kw)


def _spread(shape, grid):
    R, Cn = shape
    gi, gj = (1,) * (2 - len(grid)) + tuple(grid)
    best = None
    for shift in range(12):
        bc = -(-Cn // (LANES << shift)) * LANES
        ncol = -(-Cn // bc)
        if ncol != 1 << shift:
            continue
        nr_max = (gi * gj) // ncol
        if nr_max == 0:
            break
        br = -(-R // (nr_max * 2 * SUBLANES)) * 2 * SUBLANES
        nr = -(-R // br)
        key = (R % br == 0 and Cn % bc == 0 and 2 * nr * ncol >= gi * gj, nr * ncol)
        if best is None or key > best[0]:
            best = (key, br, bc, nr, shift)
    _, br, bc, nr, shift = best

    def index_map(*ids):
        step = ids[0] if len(ids) == 1 else ids[0] * gj + ids[1]
        s = jnp.minimum(step, (nr << shift) - 1)
        return s >> shift, s & ((1 << shift) - 1)
    return (br, bc), index_map


def _cast_specs(casts, grid):
    specs = [pl.BlockSpec(*_spread(a.shape, grid)) for a in casts]
    return specs, [jax.ShapeDtypeStruct(a.shape, BF16) for a in casts]


def _do_casts(src_refs, dst_refs):
    for src, dst in zip(src_refs, dst_refs):
        dst[...] = src[...].astype(BF16)


def _swiglu_tile(xn, wg, wu, wd, o_ref):
    n_cols = wg.shape[1]
    parts = [(c, min(c + MXU_COLS, n_cols)) for c in range(0, n_cols, MXU_COLS)]
    gate_up = [(_dot(xn, wg[:, a:b]), _dot(xn, wu[:, a:b])) for a, b in parts]
    for (a, b), (gate, up) in zip(parts, gate_up):
        act = (gate * jax.nn.sigmoid(gate) * (0.5 * up)).astype(BF16)
        o_ref[...] += _dot(act, wd[a:b, :])


def _ffn_kernel(h_ref, g_ref, wg_ref, wu_ref, wd_ref, ng_ref, *refs, final, n_casts, last_cols, has_head):
    cast_in, refs = refs[:n_casts], refs[n_casts:]
    if has_head:
        (head_o, head_u), refs = refs[:2], refs[2:]
    if final:
        o_ref, xn_ref = refs
        cast_out = ()
    elif has_head:
        o_ref, u_ref, *cast_out, xn_ref, sem = refs
    else:
        o_ref, u_ref, *cast_out, xn_ref = refs
    i, j = pl.program_id(0), pl.program_id(1)
    last = pl.num_programs(1) - 1
    active = (i > 0) if has_head else True

    if has_head:
        @pl.when((i == 0) & (j == 0))
        def _():
            copies = [pltpu.make_async_copy(head_o, o_ref, sem.at[0]), pltpu.make_async_copy(head_u, u_ref, sem.at[1])]
            for cp in copies:
                cp.start()
            for cp in copies:
                cp.wait()

        pl.when(i == 0)(lambda: _do_casts(cast_in, cast_out))

    @pl.when(active & (j == 0))
    def _():
        def init(rows):
            h = h_ref[rows, :]
            xn_ref[rows, :] = _rms(h, g_ref[...]).astype(BF16)
            o_ref[rows, :] = h
        _for_rows(h_ref.shape[0], init)

    def ff_tile(n_cols):
        _swiglu_tile(xn_ref[...], wg_ref.at[:, :n_cols], wu_ref.at[:, :n_cols], wd_ref.at[:n_cols, :], o_ref)
        _do_casts(cast_in, cast_out)

    if last_cols == wg_ref.shape[1]:
        pl.when(active)(lambda: ff_tile(last_cols))
    else:
        pl.when(active & (j < last))(lambda: ff_tile(wg_ref.shape[1]))
        pl.when(active & (j == last))(lambda: ff_tile(last_cols))

    @pl.when(active & (j == last))
    def _():
        def fin(rows):
            y = _rms(o_ref[rows, :], ng_ref[...])
            if final:
                o_ref[rows, :] = y
            else:
                u_ref[rows, :] = y.astype(BF16)
        _for_rows(h_ref.shape[0], fin)


def _ffn(h, gain, wg, wu, wd, next_gain, *, final, casts=(), head=None):
    M, D = h.shape
    F = wg.shape[1]
    tm = _tile(M, FFN_ROWS, ROW_CHUNK)
    has_head = head is not None

    def vmem_bytes(tf):
        steps = (M // tm) * -(-F // tf)
        rows = tm * D * (4 + 4 + 2 + (0 if final else 2))
        side = 0 if final else sum(2 * 6 * -(-a.size // steps) for a in casts)
        return rows + 2 * 3 * D * tf * 2 + side + 3 * tm * tf * 4

    tf = next((t for t in FFN_TILES if vmem_bytes(min(F, t)) <= V7X_VMEM_LIMIT_BYTES), FFN_TILES[-1])
    tf = min(F, tf)
    nj = -(-F // tf)
    last_cols = F - (nj - 1) * tf
    assert last_cols % LANES == 0
    grid = (M // tm, nj)
    row = lambda i, j: (i, 0)
    col = (lambda i, j: jnp.where(i == 0, 0, j)) if has_head else (lambda i, j: j)
    out_shape = [jax.ShapeDtypeStruct((M, D), F32)]
    out_specs = [_single((tm, D), row)]
    cast_specs, head_specs, scratch = [], [], [pltpu.VMEM((tm, D), BF16)]
    if not final:
        out_shape.append(jax.ShapeDtypeStruct((M, D), BF16))
        out_specs.append(_single((tm, D), row))
        cast_specs, cast_shapes = _cast_specs(casts, grid)
        out_specs += cast_specs
        out_shape += cast_shapes
    if has_head:
        head_specs = [pl.BlockSpec(memory_space=pl.ANY)] * 2
        scratch.append(pltpu.SemaphoreType.DMA((2,)))
    res = pl.pallas_call(
        functools.partial(_ffn_kernel, final=final, n_casts=len(cast_specs), last_cols=last_cols, has_head=has_head),
        grid=grid,
        in_specs=[
            _single((tm, D), (lambda i, j: (jnp.maximum(i, 1), 0)) if has_head else row),
            pl.BlockSpec((1, D), lambda i, j: (0, 0)),
            pl.BlockSpec((D, tf), lambda i, j: (0, col(i, j))),
            pl.BlockSpec((D, tf), lambda i, j: (0, col(i, j))),
            pl.BlockSpec((tf, D), lambda i, j: (col(i, j), 0)),
            pl.BlockSpec((1, D), lambda i, j: (0, 0)),
        ] + cast_specs + head_specs,
        out_specs=out_specs,
        out_shape=out_shape,
        scratch_shapes=scratch,
        compiler_params=_params("parallel", "arbitrary"),
        name="ffn_final" if final else "ffn",
    )(h, gain.reshape(1, D), wg, wu, wd, next_gain.reshape(1, D), *casts, *(head or ()))
    return res[0] if final else res


def _ffn_head_kernel(h_ref, g_ref, wg_ref, wu_ref, wd_ref, ng_ref, o_ref, u_ref, wg16_ref, wu16_ref, wd16_ref, xn_ref):
    j = pl.program_id(0)

    @pl.when(j == 0)
    def _():
        def init(rows):
            h = h_ref[rows, :]
            xn_ref[rows, :] = _rms(h, g_ref[...]).astype(BF16)
            o_ref[rows, :] = h
        _for_rows(h_ref.shape[0], init)

    wg16_ref[...] = wg_ref[...].astype(BF16)
    wu16_ref[...] = wu_ref[...].astype(BF16)
    wd16_ref[...] = wd_ref[...].astype(BF16)
    _swiglu_tile(xn_ref[...], wg16_ref, wu16_ref, wd16_ref, o_ref)

    @pl.when(j == pl.num_programs(0) - 1)
    def _():
        def fin(rows):
            u_ref[rows, :] = _rms(o_ref[rows, :], ng_ref[...]).astype(BF16)
        _for_rows(h_ref.shape[0], fin)


def _ffn_head(h, gain, wg, wu, wd, next_gain):
    M, D = h.shape
    F = wg.shape[1]
    tm = _tile(M, FFN_ROWS, ROW_CHUNK)
    tf = _tile(F, MXU_COLS)
    const = lambda j: (0, 0)
    wcol = pl.BlockSpec((D, tf), lambda j: (0, j))
    wrow = pl.BlockSpec((tf, D), lambda j: (j, 0))
    return pl.pallas_call(
        _ffn_head_kernel,
        grid=(F // tf,),
        in_specs=[_single((tm, D), const), pl.BlockSpec((1, D), const), wcol, wcol, wrow, pl.BlockSpec((1, D), const)],
        out_specs=[_single((tm, D), const), _single((tm, D), const),
                   _single((D, tf), lambda j: (0, j)), _single((D, tf), lambda j: (0, j)), _single((tf, D), lambda j: (j, 0))],
        out_shape=[jax.ShapeDtypeStruct((tm, D), F32), jax.ShapeDtypeStruct((tm, D), BF16),
                   jax.ShapeDtypeStruct(wg.shape, BF16), jax.ShapeDtypeStruct(wu.shape, BF16),
                   jax.ShapeDtypeStruct(wd.shape, BF16)],
        scratch_shapes=[pltpu.VMEM((tm, D), BF16)],
        compiler_params=_params("arbitrary"),
        name="ffn_head",
    )(h, gain.reshape(1, D), wg, wu, wd, next_gain.reshape(1, D))


CONV_ROWS = 64


def _conv_piece(z_ref, w_ref, K, r0, lanes):
    zz = z_ref[pl.ds(r0, CONV_ROWS + SUBLANES), lanes]
    acc = zz[SUBLANES:, :] * w_ref[K - 1:K, lanes]
    for s in range(1, K):
        acc = acc + pltpu.roll(zz, s, axis=0)[SUBLANES:, :] * w_ref[K - 1 - s:K - s, lanes]
    return acc


def _load_halo(z_ref, halo_ref, j, first):
    @pl.when(first)
    def _():
        z_ref[0:SUBLANES, :] = jnp.zeros((SUBLANES, z_ref.shape[1]), F32)

    @pl.when(jnp.logical_not(first))
    def _():
        z_ref[0:SUBLANES, :] = halo_ref[j]


def _proj_gate_kernel(u_ref, wb_ref, wc_ref, wx_ref, cw_ref, *refs, tiles_per_seq, K, n_casts):
    cast_in, o_ref, cast_out, (z_ref, halo_ref) = refs[:n_casts], refs[n_casts], refs[n_casts + 1:-2], refs[-2:]
    i, j = pl.program_id(0), pl.program_id(1)
    tm, tn = o_ref.shape
    _load_halo(z_ref, halo_ref, j, (i % tiles_per_seq) == 0)
    _do_casts(cast_in, cast_out)
    u = u_ref[...]
    z_ref[SUBLANES:, :] = _dot_nt(u, wc_ref[...]) * _dot_nt(u, wx_ref[...])
    halo_ref[j] = z_ref[tm:tm + SUBLANES, :]
    b = _dot_nt(u, wb_ref[...])
    for r0 in range(0, tm, CONV_ROWS):
        for c0 in range(0, tn, LANES):
            lanes = slice(c0, c0 + LANES)
            y = b[r0:r0 + CONV_ROWS, lanes] * _conv_piece(z_ref, cw_ref, K, r0, lanes)
            o_ref[r0:r0 + CONV_ROWS, lanes] = y.astype(o_ref.dtype)


def _proj_gate(u, w_t, conv_w, *, width, seq_len, casts=()):
    M, D = u.shape
    K = conv_w.shape[0]
    tm = _tile(seq_len, 1024, CONV_ROWS)
    tn = _tile(width, MXU_COLS)
    nj = width // tn
    grid = (M // tm, nj)
    wspec = lambda off: pl.BlockSpec((tn, D), lambda i, j: (j + off, 0))
    cast_specs, cast_shapes = _cast_specs(casts, grid)
    return pl.pallas_call(
        functools.partial(_proj_gate_kernel, tiles_per_seq=seq_len // tm, K=K, n_casts=len(casts)),
        grid=grid,
        in_specs=[pl.BlockSpec((tm, D), lambda i, j: (i, 0)), wspec(0), wspec(nj), wspec(2 * nj),
                  pl.BlockSpec((K, tn), lambda i, j: (0, j))] + cast_specs,
        out_specs=[pl.BlockSpec((tm, tn), lambda i, j: (i, j))] + cast_specs,
        out_shape=[jax.ShapeDtypeStruct((M, width), BF16)] + cast_shapes,
        scratch_shapes=[pltpu.VMEM((tm + SUBLANES, tn), F32), pltpu.VMEM((nj, SUBLANES, tn), F32)],
        compiler_params=_params("arbitrary", "arbitrary"),
        name="proj_gate",
    )(u, w_t, w_t, w_t, conv_w, *casts)


def _proj_conv_kernel(u_ref, w_ref, cw_ref, *refs, nj, tiles_per_seq, K, n_norm_tiles, n_casts):
    cast_in, o_ref, cast_out, (z_ref, halo_ref) = refs[:n_casts], refs[n_casts], refs[n_casts + 1:-2], refs[-2:]
    s = pl.program_id(0)
    tm, tn = o_ref.shape

    @pl.when(s == 0)
    def _():
        z_ref[...] = jnp.zeros_like(z_ref)

    t = jnp.maximum(s - 1, 0)
    i, j = t // nj, t % nj
    _load_halo(z_ref, halo_ref, j, (i % tiles_per_seq) == 0)
    normed = j < n_norm_tiles
    for r0 in range(0, tm, CONV_ROWS):
        for c0 in range(0, tn, LANES):
            lanes = slice(c0, c0 + LANES)
            acc = _conv_piece(z_ref, cw_ref, K, r0, lanes)
            y = acc * jax.nn.sigmoid(acc)
            ss = jnp.sum(y * y, axis=-1, keepdims=True)
            y = y * jnp.where(normed, lax.rsqrt(ss + L2_EPS), 1.0)
            o_ref[r0:r0 + CONV_ROWS, lanes] = y.astype(o_ref.dtype)
    halo_ref[j] = z_ref[tm:tm + SUBLANES, :]
    z_ref[SUBLANES:, :] = _dot_nt(u_ref[...], w_ref[...])
    _do_casts(cast_in, cast_out)


def _proj_conv(u, w_t, conv_w, *, row0, n_norm_cols, seq_len, casts=()):
    M, D = u.shape
    K, N = conv_w.shape
    tm = _tile(seq_len, 1024, CONV_ROWS)
    tn = _tile(N, 512)
    nj = N // tn
    steps = (M // tm) * nj
    assert row0 % tn == 0 and n_norm_cols % tn == 0
    off = row0 // tn
    cur = lambda s: jnp.minimum(s, steps - 1)
    prev = lambda s: jnp.maximum(s - 1, 0)
    cast_specs, cast_shapes = _cast_specs(casts, (steps + 1,))
    return pl.pallas_call(
        functools.partial(_proj_conv_kernel, nj=nj, tiles_per_seq=seq_len // tm, K=K, n_norm_tiles=n_norm_cols // tn,
                          n_casts=len(casts)),
        grid=(steps + 1,),
        in_specs=[pl.BlockSpec((tm, D), lambda s: (cur(s) // nj, 0)),
                  pl.BlockSpec((tn, D), lambda s: (cur(s) % nj + off, 0)),
                  pl.BlockSpec((K, tn), lambda s: (0, prev(s) % nj))] + cast_specs,
        out_specs=[pl.BlockSpec((tm, tn), lambda s: (prev(s) // nj, prev(s) % nj))] + cast_specs,
        out_shape=[jax.ShapeDtypeStruct((M, N), BF16)] + cast_shapes,
        scratch_shapes=[pltpu.VMEM((tm + SUBLANES, tn), F32), pltpu.VMEM((nj, SUBLANES, tn), F32)],
        compiler_params=_params("arbitrary"),
        name="proj_conv",
    )(u, w_t, conv_w, *casts)


def _gates_kernel(u_ref, w_ref, alog_ref, bias_ref, o_ref, *, n_heads):
    x = _dot_nt(w_ref[...], u_ref[...])
    s = x + bias_ref[...]
    softplus = jnp.maximum(s, 0.0) + jnp.log1p(jnp.exp(-jnp.abs(s)))
    g = -jnp.exp(alog_ref[...]) * softplus
    row = lax.broadcasted_iota(jnp.int32, x.shape, 0)
    o_ref[...] = jnp.where(row < n_heads, g, jax.nn.sigmoid(x))


def _gates(u, w_ab_t, a_log, dt_bias):
    M, D = u.shape
    H = a_log.shape[0]
    tm = _tile(M, 512)
    pad = jnp.zeros((H,), F32)
    col = lambda v: jnp.concatenate([v.astype(F32), pad]).reshape(2 * H, 1)
    return pl.pallas_call(
        functools.partial(_gates_kernel, n_heads=H),
        grid=(M // tm,),
        in_specs=[pl.BlockSpec((tm, D), lambda i: (i, 0)),
                  pl.BlockSpec((2 * H, D), lambda i: (0, 0)),
                  pl.BlockSpec((2 * H, 1), lambda i: (0, 0)),
                  pl.BlockSpec((2 * H, 1), lambda i: (0, 0))],
        out_specs=pl.BlockSpec((2 * H, tm), lambda i: (0, i)),
        out_shape=jax.ShapeDtypeStruct((2 * H, M), F32),
        compiler_params=_params("parallel"),
        name="gates",
    )(u, w_ab_t, col(a_log), col(dt_bias))


A_PITCH = CHUNK + SUBLANES


def _delta_kernel(q_ref, k_ref, v_ref, z_ref, g_ref, b_ref, gain_ref, o_ref,
                  gc_ref, s_ref, am_ref, at_ref, tm_ref, u_ref, wq_ref, kt_ref, a_ref, *, n_heads, n_chunks):
    C = CHUNK
    nc = n_heads * n_chunks
    scale = LANES ** -0.5
    row = lax.broadcasted_iota(jnp.int32, (C, C), 0)
    col = lax.broadcasted_iota(jnp.int32, (C, C), 1)
    eye = row == col
    causal = row >= col
    strict = row > col
    zeros_cc = jnp.zeros((C, C), F32)

    @pl.when(pl.program_id(2) == 0)
    def _():
        s_ref[...] = jnp.zeros_like(s_ref)

    prefix = jnp.where(row <= col, 1.0, 0.0)
    for hh in range(n_heads):
        gc_ref[hh] = _dot(g_ref[hh, 0], prefix, precision=HIGHEST)

    def to_col(r):
        return jnp.sum(jnp.where(eye, jnp.broadcast_to(r, (C, C)), 0.0), axis=1, keepdims=True)

    def chunk_slices(hh, c):
        return pl.ds(pl.multiple_of(c * C, C), C), slice(hh * LANES, (hh + 1) * LANES)

    def am_rows(hh, c):
        return pl.ds(pl.multiple_of((hh * n_chunks + c) * A_PITCH, SUBLANES), C)

    def intra(c, carry):
        for hh in range(n_heads):
            rows, lanes = chunk_slices(hh, c)
            k16 = k_ref[rows, lanes]
            k = k16.astype(F32)
            q = q_ref[rows, lanes].astype(F32) * scale
            g_row = gc_ref[hh, pl.ds(c, 1), :]
            g_col = to_col(g_row)
            b_col = to_col(b_ref[hh, 0, pl.ds(c, 1), :])
            decay = jnp.where(causal, jnp.exp(jnp.where(causal, g_col - g_row, 0.0)), 0.0)
            prod = _dot_nt(jnp.concatenate([k * b_col, q], axis=0).astype(BF16), k16)
            a = jnp.where(strict, prod[:C] * decay, 0.0)
            am_ref[am_rows(hh, c), :] = jnp.concatenate([a, zeros_cc], axis=1)
            a_ref[hh, rows, :] = (prod[C:] * decay).astype(BF16)
        return carry

    lax.fori_loop(0, n_chunks, intra, 0)

    for i in range(C):
        at_ref[i] = am_ref[pl.ds(i, nc, stride=A_PITCH), :].T[:C, :]

    zero_blk = jnp.zeros((SUBLANES, nc), F32)
    for i in range(C):
        nb = -(-i // SUBLANES)
        acc = [-at_ref[i, b * SUBLANES:(b + 1) * SUBLANES, :] for b in range(nb)]
        for j in range(1, i):
            a_ij = jnp.broadcast_to(at_ref[i, j:j + 1, :], (SUBLANES, nc))
            for b in range(-(-j // SUBLANES)):
                acc[b] = acc[b] - a_ij * tm_ref[j, b * SUBLANES:(b + 1) * SUBLANES, :]
        for b in range(C // SUBLANES):
            tm_ref[i, b * SUBLANES:(b + 1) * SUBLANES, :] = acc[b] if b < nb else zero_blk

    zeros_pad = jnp.zeros((LANES - C, nc), F32)
    for i in range(C):
        am_ref[pl.ds(i, nc, stride=A_PITCH), :] = jnp.concatenate([tm_ref[i], zeros_pad], axis=0).T

    def solve(c, carry):
        for hh in range(n_heads):
            rows, lanes = chunk_slices(hh, c)
            k = k_ref[rows, lanes].astype(F32)
            v = v_ref[rows, lanes].astype(F32)
            q = q_ref[rows, lanes].astype(F32) * scale
            g_row = gc_ref[hh, pl.ds(c, 1), :]
            g_col = to_col(g_row)
            b_col = to_col(b_ref[hh, 0, pl.ds(c, 1), :])
            e_col = jnp.exp(g_col)
            rhs = jnp.concatenate([v * b_col, k * (b_col * e_col)], axis=1)
            tm = am_ref[am_rows(hh, c), :][:, :C]
            sol = rhs + _dot(tm.astype(BF16), rhs.astype(BF16))
            u_ref[hh, rows, :] = sol[:, :LANES]
            wq_ref[hh, pl.ds(pl.multiple_of(c * 2 * C, 2 * C), C), :] = sol[:, LANES:].astype(BF16)
            wq_ref[hh, pl.ds(pl.multiple_of(c * 2 * C + C, C), C), :] = (q * e_col).astype(BF16)
            k_dec = k * jnp.exp(g_row[:, C - 1:C] - g_col)
            kt_ref[hh, pl.ds(pl.multiple_of(c * LANES, LANES), LANES), :] = k_dec.T.astype(BF16)
        return carry

    lax.fori_loop(0, n_chunks, solve, 0)

    def recur(c, carry):
        heads = range(n_heads)
        rows = pl.ds(pl.multiple_of(c * C, C), C)
        S = [s_ref[hh] for hh in heads]
        r = [_dot(wq_ref[hh, pl.ds(pl.multiple_of(c * 2 * C, 2 * C), 2 * C), :], S[hh].astype(BF16))
             for hh in heads]
        v16 = [(u_ref[hh, rows, :] - r[hh][:C]).astype(BF16) for hh in heads]
        for hh in heads:
            g_last = gc_ref[hh, pl.ds(c, 1), :][:, C - 1:C]
            kt = kt_ref[hh, pl.ds(pl.multiple_of(c * LANES, LANES), LANES), :]
            s_ref[hh] = S[hh] * jnp.exp(g_last) + _dot(kt, v16[hh])
        for hh in heads:
            lanes = slice(hh * LANES, (hh + 1) * LANES)
            o = r[hh][C:] + _dot(a_ref[hh, rows, :], v16[hh])
            o_ref[rows, lanes] = (_rms(o, gain_ref[...]) * z_ref[rows, lanes].astype(F32)).astype(o_ref.dtype)
        return carry

    lax.fori_loop(0, n_chunks, recur, 0)


def _delta(qkvz, gb, gain, *, n_heads, seq_len):
    M = qkvz.shape[0]
    W = n_heads * LANES
    B = M // seq_len
    nh = min(n_heads, 8)
    ncb = LANES // nh
    tb = ncb * CHUNK
    hw = nh * LANES
    nb = W // hw
    nt = seq_len // tb
    assert LANES % nh == 0 and n_heads % nh == 0 and seq_len % tb == 0 and ncb % SUBLANES == 0
    tok = lambda off: pl.BlockSpec((tb, hw), lambda b, h, t: (b * nt + t, h + off))
    return pl.pallas_call(
        functools.partial(_delta_kernel, n_heads=nh, n_chunks=ncb),
        grid=(B, nb, nt),
        in_specs=[tok(0), tok(nb), tok(2 * nb), tok(3 * nb),
                  pl.BlockSpec((nh, 1, ncb, CHUNK), lambda b, h, t: (h, b, t, 0)),
                  pl.BlockSpec((nh, 1, ncb, CHUNK), lambda b, h, t: (h + nb, b, t, 0)),
                  pl.BlockSpec((1, LANES), lambda b, h, t: (0, 0))],
        out_specs=tok(0),
        out_shape=jax.ShapeDtypeStruct((M, W), BF16),
        scratch_shapes=[pltpu.VMEM((nh, ncb, CHUNK), F32),
                        pltpu.VMEM((nh, LANES, LANES), F32),
                        pltpu.VMEM((LANES * A_PITCH, LANES), F32),
                        pltpu.VMEM((CHUNK, CHUNK, LANES), F32),
                        pltpu.VMEM((CHUNK, CHUNK, LANES), F32),
                        pltpu.VMEM((nh, tb, LANES), F32),
                        pltpu.VMEM((nh, 2 * tb, LANES), BF16),
                        pltpu.VMEM((nh, ncb * LANES, CHUNK), BF16),
                        pltpu.VMEM((nh, tb, CHUNK), BF16)],
        compiler_params=_params("parallel", "parallel", "arbitrary"),
        name="delta_rule",
    )(qkvz, qkvz, qkvz, qkvz, gb, gb, gain.reshape(1, LANES))


def _merge_kernel(u_ref, yc_ref, yd_ref, h_ref, wgc_ref, wgd_ref, wc_ref, wdn_ref, wo_ref, o_ref):
    @pl.when(pl.program_id(1) == 0)
    def _():
        def init(rows):
            o_ref[rows, :] = h_ref[rows, :]
        _for_rows(h_ref.shape[0], init)

    u = u_ref[...]
    gc = jax.nn.sigmoid(_dot_nt(u, wgc_ref[...]))
    gd = jax.nn.sigmoid(_dot_nt(u, wgd_ref[...]))
    merged = gc * _dot(yc_ref[...], wc_ref[...]) + gd * _dot(yd_ref[...], wdn_ref[...])
    o_ref[...] += _dot(merged.astype(BF16), wo_ref[...])


def _merge(u, y_conv, y_dn, h, wg_t, wc, wdn, wo):
    M, D = h.shape
    Wc = y_conv.shape[1]
    Wd = y_dn.shape[1]
    tm = _tile(M, 512, ROW_CHUNK)
    tn = _tile(D, 256)
    nn = D // tn
    row = lambda i, j: (i, 0)
    colw = lambda i, j: (0, j)
    return pl.pallas_call(
        _merge_kernel,
        grid=(M // tm, nn),
        in_specs=[pl.BlockSpec((tm, D), row), pl.BlockSpec((tm, Wc), row), pl.BlockSpec((tm, Wd), row),
                  _single((tm, D), row),
                  pl.BlockSpec((tn, D), lambda i, j: (j, 0)), pl.BlockSpec((tn, D), lambda i, j: (j + nn, 0)),
                  pl.BlockSpec((Wc, tn), colw), pl.BlockSpec((Wd, tn), colw),
                  pl.BlockSpec((tn, D), lambda i, j: (j, 0))],
        out_specs=_single((tm, D), row),
        out_shape=jax.ShapeDtypeStruct((M, D), F32),
        compiler_params=_params("parallel", "arbitrary"),
        name="merge_out",
    )(u, y_conv, y_dn, h, wg_t, wg_t, wc, wdn, wo)


def kernel(x, ffn1_norm, ffn1_w_gate, ffn1_w_up, ffn1_w_down, mix_norm, w_in, conv_mixer_w, dn_conv_w, dn_a_log, dn_dt_bias, dn_out_norm, w_conv_branch, w_dn_branch, w_out, ffn2_norm, ffn2_w_gate, ffn2_w_up, ffn2_w_down, final_norm):
    B, T, D = x.shape
    M = B * T
    depth = ffn1_norm.shape[0]
    Wc = conv_mixer_w.shape[1]
    Wd = dn_conv_w.shape[1] // 3
    H = dn_a_log.shape[1]
    assert Wd == H * LANES and dn_out_norm.shape[1] == LANES and T % CHUNK == 0
    c_q = 3 * Wc
    c_z = c_q + 3 * Wd
    c_a = c_z + Wd
    c_gc = c_a + 2 * H
    c_gd = c_gc + D
    assert w_in.shape[2] == c_gd + D
    bf = lambda w: w.astype(BF16)

    h = x.reshape(M, D)
    for l in range(depth):
        last = l == depth - 1
        *head, wg1, wu1, wd1 = _ffn_head(h, ffn1_norm[l], ffn1_w_gate[l], ffn1_w_up[l], ffn1_w_down[l], mix_norm[l])
        h, u, w_in16 = _ffn(h, ffn1_norm[l], wg1, wu1, wd1, mix_norm[l], final=False, casts=(w_in[l].T,), head=head)

        gb = _gates(u, w_in16[c_a:c_gc], dn_a_log[l], dn_dt_bias[l])
        y_conv, wc16, wdn16, wo16 = _proj_gate(u, w_in16, conv_mixer_w[l].T, width=Wc, seq_len=T,
                                               casts=(w_conv_branch[l], w_dn_branch[l], w_out[l]))
        taps = dn_conv_w.shape[2]
        identity = jnp.zeros((taps, Wd), F32).at[taps - 1].set(1.0)
        qkvz, wg2, wu2, wd2 = _proj_conv(u, w_in16, jnp.concatenate([dn_conv_w[l].T, identity], axis=1),
                                         row0=c_q, n_norm_cols=2 * Wd, seq_len=T,
                                         casts=(ffn2_w_gate[l], ffn2_w_up[l], ffn2_w_down[l]))
        y_dn = _delta(qkvz, gb.reshape(2 * H, B, T // CHUNK, CHUNK), dn_out_norm[l], n_heads=H, seq_len=T)
        h = _merge(u, y_conv, y_dn, h, w_in16[c_gc:], wc16, wdn16, wo16)

        if last:
            h = _ffn(h, ffn2_norm[l], wg2, wu2, wd2, final_norm, final=True)
        else:
            h = _ffn(h, ffn2_norm[l], wg2, wu2, wd2, final_norm, final=False)[0]
    return h.reshape(B, T, D)
```

```python
import functools

import jax
import jax.numpy as jnp
from jax import lax
from jax.experimental import pallas as pl
from jax.experimental.pallas import tpu as pltpu

F32 = jnp.float32
BF16 = jnp.bfloat16
EPS = 1e-6
L2_EPS = 1e-6
CHUNK = 64
LANES = 128
SUBLANES = 8
MXU_COLS = 256
ROW_CHUNK = 32
FFN_ROWS = 512
FFN_TILES = (512, 256)
V7X_VMEM_LIMIT_BYTES = 56 * 1024 * 1024
HIGHEST = lax.Precision.HIGHEST


def _params(*sem):
    return pltpu.CompilerParams(dimension_semantics=sem, vmem_limit_bytes=V7X_VMEM_LIMIT_BYTES)


def _tile(n, pref, mult=LANES):
    if n <= pref:
        return n
    t = (pref // mult) * mult
    while t > 0 and n % t:
        t -= mult
    assert t > 0, (n, pref, mult)
    return t


def _single(block, index_map):
    return pl.BlockSpec(block, index_map, pipeline_mode=pl.Buffered(1))


def _rms(h, gain):
    return h * lax.rsqrt(jnp.mean(h * h, axis=-1, keepdims=True) + EPS) * gain


def _for_rows(n_rows, fn):
    def body(r, carry):
        fn(pl.ds(pl.multiple_of(r * ROW_CHUNK, ROW_CHUNK), ROW_CHUNK))
        return carry
    lax.fori_loop(0, n_rows // ROW_CHUNK, body, 0)


def _dot(a, b, **kw):
    return jnp.dot(a, b, preferred_element_type=F32, **kw)


def _dot_nt(a, b, **kw):
    return lax.dot_general(a, b, (((1,), (1,)), ((), ())), preferred_element_type=F32, **kw)


def _dot_tn(a, b, **kw):
    return lax.dot_general(a, b, (((0,), (0,)), ((), ())), preferred_element_type=F32, **kw)


def _spread(shape, grid, row0=0):
    R, Cn = shape
    gi, gj = (1,) * (2 - len(grid)) + tuple(grid)
    best = None
    for shift in range(12):
        bc = -(-Cn // (LANES << shift)) * LANES
        ncol = -(-Cn // bc)
        if ncol != 1 << shift:
            continue
        nr_max = (gi * gj) // ncol
        if nr_max == 0:
            break
        br = -(-R // (nr_max * 2 * SUBLANES)) * 2 * SUBLANES
        nr = -(-R // br)
        if row0 % br:
            continue
        key = (R % br == 0 and Cn % bc == 0 and 2 * nr * ncol >= gi * gj, nr * ncol)
        if best is None or key > best[0]:
            best = (key, br, bc, nr, shift)
    assert best is not None, f"no (16a, 128b) blocking of {shape} from row {row0} fits {gi * gj} steps"
    _, br, bc, nr, shift = best

    def index_map(*ids, offset=0):
        step = ids[0] if len(ids) == 1 else ids[0] * gj + ids[1]
        s = jnp.minimum(step, (nr << shift) - 1)
        return (s >> shift) + offset, s & ((1 << shift) - 1)
    return (br, bc), index_map, functools.partial(index_map, offset=row0 // br)


def _cast_specs(casts, grid):
    arrays, in_specs, out_specs, shapes = [], [], [], []
    for c in casts:
        a, row0, n_rows = c if isinstance(c, tuple) else (c, 0, c.shape[0])
        block, out_map, in_map = _spread((n_rows, a.shape[1]), grid, row0)
        arrays.append(a)
        in_specs.append(pl.BlockSpec(block, in_map))
        out_specs.append(pl.BlockSpec(block, out_map))
        shapes.append(jax.ShapeDtypeStruct((n_rows, a.shape[1]), BF16))
    return arrays, in_specs, out_specs, shapes


def _do_casts(src_refs, dst_refs):
    for src, dst in zip(src_refs, dst_refs):
        dst[...] = src[...].astype(BF16)


def _swiglu_tile(xn, wg, wu, wd, o_ref):
    n_cols = wg.shape[1]
    parts = [(c, min(c + MXU_COLS, n_cols)) for c in range(0, n_cols, MXU_COLS)]
    gate_up = [(_dot(xn, wg[:, a:b]), _dot(xn, wu[:, a:b])) for a, b in parts]
    for (a, b), (gate, up) in zip(parts, gate_up):
        act = (gate * jax.nn.sigmoid(gate) * (0.5 * up)).astype(BF16)
        o_ref[...] += _dot(act, wd[a:b, :])


def _ffn_kernel(h_ref, g_ref, wg_ref, wu_ref, wd_ref, ng_ref, *refs, final, n_casts, last_cols, has_head):
    cast_in, refs = refs[:n_casts], refs[n_casts:]
    if has_head:
        (head_o, head_u), refs = refs[:2], refs[2:]
    if final:
        o_ref, xn_ref = refs
        cast_out = ()
    elif has_head:
        o_ref, u_ref, *cast_out, xn_ref, sem = refs
    else:
        o_ref, u_ref, *cast_out, xn_ref = refs
    i, j = pl.program_id(0), pl.program_id(1)
    last = pl.num_programs(1) - 1
    active = (i > 0) if has_head else True

    if has_head:
        @pl.when((i == 0) & (j == 0))
        def _():
            copies = [pltpu.make_async_copy(head_o, o_ref, sem.at[0]), pltpu.make_async_copy(head_u, u_ref, sem.at[1])]
            for cp in copies:
                cp.start()
            for cp in copies:
                cp.wait()

        pl.when(i == 0)(lambda: _do_casts(cast_in, cast_out))

    @pl.when(active & (j == 0))
    def _():
        def init(rows):
            h = h_ref[rows, :]
            xn_ref[rows, :] = _rms(h, g_ref[...]).astype(BF16)
            o_ref[rows, :] = h
        _for_rows(h_ref.shape[0], init)

    def ff_tile(n_cols):
        _swiglu_tile(xn_ref[...], wg_ref.at[:, :n_cols], wu_ref.at[:, :n_cols], wd_ref.at[:n_cols, :], o_ref)
        _do_casts(cast_in, cast_out)

    if last_cols == wg_ref.shape[1]:
        pl.when(active)(lambda: ff_tile(last_cols))
    else:
        pl.when(active & (j < last))(lambda: ff_tile(wg_ref.shape[1]))
        pl.when(active & (j == last))(lambda: ff_tile(last_cols))

    @pl.when(active & (j == last))
    def _():
        def fin(rows):
            y = _rms(o_ref[rows, :], ng_ref[...])
            if final:
                o_ref[rows, :] = y
            else:
                u_ref[rows, :] = y.astype(BF16)
        _for_rows(h_ref.shape[0], fin)


def _ffn(h, gain, wg, wu, wd, next_gain, *, final, casts=(), head=None):
    M, D = h.shape
    F = wg.shape[1]
    tm = _tile(M, FFN_ROWS, ROW_CHUNK)
    has_head = head is not None

    def vmem_bytes(tf):
        steps = (M // tm) * -(-F // tf)
        rows = tm * D * (4 + 4 + 2 + (0 if final else 2))
        elems = [c[2] * c[0].shape[1] if isinstance(c, tuple) else c.size for c in casts]
        side = 0 if final else sum(2 * 6 * -(-n // steps) for n in elems)
        return rows + 2 * 3 * D * tf * 2 + side + 3 * tm * tf * 4

    tf = next((t for t in FFN_TILES if vmem_bytes(min(F, t)) <= V7X_VMEM_LIMIT_BYTES), FFN_TILES[-1])
    tf = min(F, tf)
    nj = -(-F // tf)
    last_cols = F - (nj - 1) * tf
    assert last_cols % LANES == 0
    grid = (M // tm, nj)
    row = lambda i, j: (i, 0)
    col = (lambda i, j: jnp.where(i == 0, 0, j)) if has_head else (lambda i, j: j)
    out_shape = [jax.ShapeDtypeStruct((M, D), F32)]
    out_specs = [_single((tm, D), row)]
    cast_specs, head_specs, scratch = [], [], [pltpu.VMEM((tm, D), BF16)]
    if not final:
        out_shape.append(jax.ShapeDtypeStruct((M, D), BF16))
        out_specs.append(_single((tm, D), row))
        casts, cast_specs, cast_out_specs, cast_shapes = _cast_specs(casts, grid)
        out_specs += cast_out_specs
        out_shape += cast_shapes
    if has_head:
        head_specs = [pl.BlockSpec(memory_space=pl.ANY)] * 2
        scratch.append(pltpu.SemaphoreType.DMA((2,)))
    res = pl.pallas_call(
        functools.partial(_ffn_kernel, final=final, n_casts=len(cast_specs), last_cols=last_cols, has_head=has_head),
        grid=grid,
        in_specs=[
            _single((tm, D), (lambda i, j: (jnp.maximum(i, 1), 0)) if has_head else row),
            pl.BlockSpec((1, D), lambda i, j: (0, 0)),
            pl.BlockSpec((D, tf), lambda i, j: (0, col(i, j))),
            pl.BlockSpec((D, tf), lambda i, j: (0, col(i, j))),
            pl.BlockSpec((tf, D), lambda i, j: (col(i, j), 0)),
            pl.BlockSpec((1, D), lambda i, j: (0, 0)),
        ] + cast_specs + head_specs,
        out_specs=out_specs,
        out_shape=out_shape,
        scratch_shapes=scratch,
        compiler_params=_params("parallel", "arbitrary"),
        name="ffn_final" if final else "ffn",
    )(h, gain.reshape(1, D), wg, wu, wd, next_gain.reshape(1, D), *casts, *(head or ()))
    return res[0] if final else res


def _ffn_head_kernel(h_ref, g_ref, wg_ref, wu_ref, wd_ref, ng_ref, o_ref, u_ref, wg16_ref, wu16_ref, wd16_ref, xn_ref):
    j = pl.program_id(0)

    @pl.when(j == 0)
    def _():
        def init(rows):
            h = h_ref[rows, :]
            xn_ref[rows, :] = _rms(h, g_ref[...]).astype(BF16)
            o_ref[rows, :] = h
        _for_rows(h_ref.shape[0], init)

    wg16_ref[...] = wg_ref[...].astype(BF16)
    wu16_ref[...] = wu_ref[...].astype(BF16)
    wd16_ref[...] = wd_ref[...].astype(BF16)
    _swiglu_tile(xn_ref[...], wg16_ref, wu16_ref, wd16_ref, o_ref)

    @pl.when(j == pl.num_programs(0) - 1)
    def _():
        def fin(rows):
            u_ref[rows, :] = _rms(o_ref[rows, :], ng_ref[...]).astype(BF16)
        _for_rows(h_ref.shape[0], fin)


def _ffn_head(h, gain, wg, wu, wd, next_gain):
    M, D = h.shape
    F = wg.shape[1]
    tm = _tile(M, FFN_ROWS, ROW_CHUNK)
    tf = _tile(F, MXU_COLS)
    const = lambda j: (0, 0)
    wcol = pl.BlockSpec((D, tf), lambda j: (0, j))
    wrow = pl.BlockSpec((tf, D), lambda j: (j, 0))
    return pl.pallas_call(
        _ffn_head_kernel,
        grid=(F // tf,),
        in_specs=[_single((tm, D), const), pl.BlockSpec((1, D), const), wcol, wcol, wrow, pl.BlockSpec((1, D), const)],
        out_specs=[_single((tm, D), const), _single((tm, D), const),
                   _single((D, tf), lambda j: (0, j)), _single((D, tf), lambda j: (0, j)), _single((tf, D), lambda j: (j, 0))],
        out_shape=[jax.ShapeDtypeStruct((tm, D), F32), jax.ShapeDtypeStruct((tm, D), BF16),
                   jax.ShapeDtypeStruct(wg.shape, BF16), jax.ShapeDtypeStruct(wu.shape, BF16),
                   jax.ShapeDtypeStruct(wd.shape, BF16)],
        scratch_shapes=[pltpu.VMEM((tm, D), BF16)],
        compiler_params=_params("arbitrary"),
        name="ffn_head",
    )(h, gain.reshape(1, D), wg, wu, wd, next_gain.reshape(1, D))


CONV_ROWS = 64


def _conv_piece(z_ref, w_ref, K, r0, lanes):
    zz = z_ref[pl.ds(r0, CONV_ROWS + SUBLANES), lanes]
    acc = zz[SUBLANES:, :] * w_ref[K - 1:K, lanes]
    for s in range(1, K):
        acc = acc + pltpu.roll(zz, s, axis=0)[SUBLANES:, :] * w_ref[K - 1 - s:K - s, lanes]
    return acc


def _load_halo(z_ref, halo_ref, j, first):
    @pl.when(first)
    def _():
        z_ref[0:SUBLANES, :] = jnp.zeros((SUBLANES, z_ref.shape[1]), F32)

    @pl.when(jnp.logical_not(first))
    def _():
        z_ref[0:SUBLANES, :] = halo_ref[j]


def _proj_gate_kernel(u_ref, wb_ref, wc_ref, wx_ref, cw_ref, *refs, tiles_per_seq, K, n_casts):
    cast_in, o_ref, cast_out, (z_ref, halo_ref) = refs[:n_casts], refs[n_casts], refs[n_casts + 1:-2], refs[-2:]
    i, j = pl.program_id(0), pl.program_id(1)
    tm, tn = o_ref.shape
    _load_halo(z_ref, halo_ref, j, (i % tiles_per_seq) == 0)
    _do_casts(cast_in, cast_out)
    u = u_ref[...]
    z_ref[SUBLANES:, :] = _dot_nt(u, wc_ref[...]) * _dot_nt(u, wx_ref[...])
    halo_ref[j] = z_ref[tm:tm + SUBLANES, :]
    b = _dot_nt(u, wb_ref[...])
    for r0 in range(0, tm, CONV_ROWS):
        for c0 in range(0, tn, LANES):
            lanes = slice(c0, c0 + LANES)
            y = b[r0:r0 + CONV_ROWS, lanes] * _conv_piece(z_ref, cw_ref, K, r0, lanes)
            o_ref[r0:r0 + CONV_ROWS, lanes] = y.astype(o_ref.dtype)


def _proj_gate(u, w_t, conv_w, *, width, seq_len, casts=()):
    M, D = u.shape
    K = conv_w.shape[0]
    tm = _tile(seq_len, 1024, CONV_ROWS)
    tn = _tile(width, MXU_COLS)
    nj = width // tn
    grid = (M // tm, nj)
    wspec = lambda off: pl.BlockSpec((tn, D), lambda i, j: (j + off, 0))
    casts, cast_specs, cast_out_specs, cast_shapes = _cast_specs(casts, grid)
    return pl.pallas_call(
        functools.partial(_proj_gate_kernel, tiles_per_seq=seq_len // tm, K=K, n_casts=len(casts)),
        grid=grid,
        in_specs=[pl.BlockSpec((tm, D), lambda i, j: (i, 0)), wspec(0), wspec(nj), wspec(2 * nj),
                  pl.BlockSpec((K, tn), lambda i, j: (0, j))] + cast_specs,
        out_specs=[pl.BlockSpec((tm, tn), lambda i, j: (i, j))] + cast_out_specs,
        out_shape=[jax.ShapeDtypeStruct((M, width), BF16)] + cast_shapes,
        scratch_shapes=[pltpu.VMEM((tm + SUBLANES, tn), F32), pltpu.VMEM((nj, SUBLANES, tn), F32)],
        compiler_params=_params("arbitrary", "arbitrary"),
        name="proj_gate",
    )(u, w_t, w_t, w_t, conv_w, *casts)


def _proj_conv_kernel(u_ref, w_ref, cw_ref, *refs, nj, tiles_per_seq, K, n_norm_tiles, n_casts):
    cast_in, o_ref, cast_out, (z_ref, halo_ref) = refs[:n_casts], refs[n_casts], refs[n_casts + 1:-2], refs[-2:]
    s = pl.program_id(0)
    tm, tn = o_ref.shape

    @pl.when(s == 0)
    def _():
        z_ref[...] = jnp.zeros_like(z_ref)

    t = jnp.maximum(s - 1, 0)
    i, j = t // nj, t % nj
    _load_halo(z_ref, halo_ref, j, (i % tiles_per_seq) == 0)
    normed = j < n_norm_tiles
    for r0 in range(0, tm, CONV_ROWS):
        for c0 in range(0, tn, LANES):
            lanes = slice(c0, c0 + LANES)
            acc = _conv_piece(z_ref, cw_ref, K, r0, lanes)
            y = acc * jax.nn.sigmoid(acc)
            ss = jnp.sum(y * y, axis=-1, keepdims=True)
            y = y * jnp.where(normed, lax.rsqrt(ss + L2_EPS), 1.0)
            o_ref[r0:r0 + CONV_ROWS, lanes] = y.astype(o_ref.dtype)
    halo_ref[j] = z_ref[tm:tm + SUBLANES, :]
    z_ref[SUBLANES:, :] = _dot_nt(u_ref[...], w_ref[...])
    _do_casts(cast_in, cast_out)


def _proj_conv(u, w_t, conv_w, *, row0, n_norm_cols, seq_len, casts=()):
    M, D = u.shape
    K, N = conv_w.shape
    tm = _tile(seq_len, 1024, CONV_ROWS)
    tn = _tile(N, 512)
    nj = N // tn
    steps = (M // tm) * nj
    assert row0 % tn == 0 and n_norm_cols % tn == 0
    off = row0 // tn
    cur = lambda s: jnp.minimum(s, steps - 1)
    prev = lambda s: jnp.maximum(s - 1, 0)
    casts, cast_specs, cast_out_specs, cast_shapes = _cast_specs(casts, (steps + 1,))
    return pl.pallas_call(
        functools.partial(_proj_conv_kernel, nj=nj, tiles_per_seq=seq_len // tm, K=K, n_norm_tiles=n_norm_cols // tn,
                          n_casts=len(casts)),
        grid=(steps + 1,),
        in_specs=[pl.BlockSpec((tm, D), lambda s: (cur(s) // nj, 0)),
                  pl.BlockSpec((tn, D), lambda s: (cur(s) % nj + off, 0)),
                  pl.BlockSpec((K, tn), lambda s: (0, prev(s) % nj))] + cast_specs,
        out_specs=[pl.BlockSpec((tm, tn), lambda s: (prev(s) // nj, prev(s) % nj))] + cast_out_specs,
        out_shape=[jax.ShapeDtypeStruct((M, N), BF16)] + cast_shapes,
        scratch_shapes=[pltpu.VMEM((tm + SUBLANES, tn), F32), pltpu.VMEM((nj, SUBLANES, tn), F32)],
        compiler_params=_params("arbitrary"),
        name="proj_conv",
    )(u, w_t, conv_w, *casts)


def _gates_kernel(u_ref, w_ref, alog_ref, bias_ref, o_ref, *, n_heads):
    x = _dot_nt(w_ref[...], u_ref[...])
    s = x + bias_ref[...]
    softplus = jnp.maximum(s, 0.0) + jnp.log1p(jnp.exp(-jnp.abs(s)))
    g = -jnp.exp(alog_ref[...]) * softplus
    row = lax.broadcasted_iota(jnp.int32, x.shape, 0)
    o_ref[...] = jnp.where(row < n_heads, g, jax.nn.sigmoid(x))


def _gates(u, w_ab_t, a_log, dt_bias):
    M, D = u.shape
    H = a_log.shape[0]
    tm = _tile(M, 512)
    pad = jnp.zeros((H,), F32)
    col = lambda v: jnp.concatenate([v.astype(F32), pad]).reshape(2 * H, 1)
    return pl.pallas_call(
        functools.partial(_gates_kernel, n_heads=H),
        grid=(M // tm,),
        in_specs=[pl.BlockSpec((tm, D), lambda i: (i, 0)),
                  pl.BlockSpec((2 * H, D), lambda i: (0, 0)),
                  pl.BlockSpec((2 * H, 1), lambda i: (0, 0)),
                  pl.BlockSpec((2 * H, 1), lambda i: (0, 0))],
        out_specs=pl.BlockSpec((2 * H, tm), lambda i: (0, i)),
        out_shape=jax.ShapeDtypeStruct((2 * H, M), F32),
        compiler_params=_params("parallel"),
        name="gates",
    )(u, w_ab_t, col(a_log), col(dt_bias))


A_PITCH = CHUNK + SUBLANES


def _delta_kernel(q_ref, k_ref, v_ref, z_ref, g_ref, b_ref, gain_ref, o_ref,
                  gc_ref, s_ref, am_ref, at_ref, tm_ref, u_ref, wq_ref, kt_ref, a_ref, *, n_heads, n_chunks):
    C = CHUNK
    nc = n_heads * n_chunks
    scale = LANES ** -0.5
    row = lax.broadcasted_iota(jnp.int32, (C, C), 0)
    col = lax.broadcasted_iota(jnp.int32, (C, C), 1)
    eye = row == col
    causal = row >= col
    strict = row > col
    zeros_cc = jnp.zeros((C, C), F32)

    @pl.when(pl.program_id(2) == 0)
    def _():
        s_ref[...] = jnp.zeros_like(s_ref)

    prefix = jnp.where(row <= col, 1.0, 0.0)
    for hh in range(n_heads):
        gc_ref[hh] = _dot(g_ref[hh, 0], prefix, precision=HIGHEST)

    def to_col(r):
        return jnp.sum(jnp.where(eye, jnp.broadcast_to(r, (C, C)), 0.0), axis=1, keepdims=True)

    def chunk_slices(hh, c):
        return pl.ds(pl.multiple_of(c * C, C), C), slice(hh * LANES, (hh + 1) * LANES)

    def am_rows(hh, c):
        return pl.ds(pl.multiple_of((hh * n_chunks + c) * A_PITCH, SUBLANES), C)

    def intra(c, carry):
        for hh in range(n_heads):
            rows, lanes = chunk_slices(hh, c)
            k16 = k_ref[rows, lanes]
            k = k16.astype(F32)
            q = q_ref[rows, lanes].astype(F32) * scale
            g_row = gc_ref[hh, pl.ds(c, 1), :]
            g_col = to_col(g_row)
            b_col = to_col(b_ref[hh, 0, pl.ds(c, 1), :])
            decay = jnp.where(causal, jnp.exp(jnp.where(causal, g_col - g_row, 0.0)), 0.0)
            prod = _dot_nt(jnp.concatenate([k * b_col, q], axis=0).astype(BF16), k16)
            a = jnp.where(strict, prod[:C] * decay, 0.0)
            am_ref[am_rows(hh, c), :] = jnp.concatenate([a, zeros_cc], axis=1)
            a_ref[hh, rows, :] = (prod[C:] * decay).astype(BF16)
        return carry

    lax.fori_loop(0, n_chunks, intra, 0)

    for i in range(C):
        at_ref[i] = am_ref[pl.ds(i, nc, stride=A_PITCH), :].T[:C, :]

    zero_blk = jnp.zeros((SUBLANES, nc), F32)
    for i in range(C):
        nb = -(-i // SUBLANES)
        acc = [-at_ref[i, b * SUBLANES:(b + 1) * SUBLANES, :] for b in range(nb)]
        for j in range(1, i):
            a_ij = jnp.broadcast_to(at_ref[i, j:j + 1, :], (SUBLANES, nc))
            for b in range(-(-j // SUBLANES)):
                acc[b] = acc[b] - a_ij * tm_ref[j, b * SUBLANES:(b + 1) * SUBLANES, :]
        for b in range(C // SUBLANES):
            tm_ref[i, b * SUBLANES:(b + 1) * SUBLANES, :] = acc[b] if b < nb else zero_blk

    zeros_pad = jnp.zeros((LANES - C, nc), F32)
    for i in range(C):
        am_ref[pl.ds(i, nc, stride=A_PITCH), :] = jnp.concatenate([tm_ref[i], zeros_pad], axis=0).T

    def solve(c, carry):
        for hh in range(n_heads):
            rows, lanes = chunk_slices(hh, c)
            k = k_ref[rows, lanes].astype(F32)
            v = v_ref[rows, lanes].astype(F32)
            q = q_ref[rows, lanes].astype(F32) * scale
            g_row = gc_ref[hh, pl.ds(c, 1), :]
            g_col = to_col(g_row)
            b_col = to_col(b_ref[hh, 0, pl.ds(c, 1), :])
            e_col = jnp.exp(g_col)
            rhs = jnp.concatenate([v * b_col, k * (b_col * e_col)], axis=1)
            tm = am_ref[am_rows(hh, c), :][:, :C]
            sol = rhs + _dot(tm.astype(BF16), rhs.astype(BF16))
            u_ref[hh, rows, :] = sol[:, :LANES]
            wq_ref[hh, pl.ds(pl.multiple_of(c * 2 * C, 2 * C), C), :] = sol[:, LANES:].astype(BF16)
            wq_ref[hh, pl.ds(pl.multiple_of(c * 2 * C + C, C), C), :] = (q * e_col).astype(BF16)
            k_dec = k * jnp.exp(g_row[:, C - 1:C] - g_col)
            kt_ref[hh, pl.ds(pl.multiple_of(c * LANES, LANES), LANES), :] = k_dec.T.astype(BF16)
        return carry

    lax.fori_loop(0, n_chunks, solve, 0)

    def recur(c, carry):
        heads = range(n_heads)
        rows = pl.ds(pl.multiple_of(c * C, C), C)
        S = [s_ref[hh] for hh in heads]
        r = [_dot(wq_ref[hh, pl.ds(pl.multiple_of(c * 2 * C, 2 * C), 2 * C), :], S[hh].astype(BF16))
             for hh in heads]
        v16 = [(u_ref[hh, rows, :] - r[hh][:C]).astype(BF16) for hh in heads]
        for hh in heads:
            g_last = gc_ref[hh, pl.ds(c, 1), :][:, C - 1:C]
            kt = kt_ref[hh, pl.ds(pl.multiple_of(c * LANES, LANES), LANES), :]
            s_ref[hh] = S[hh] * jnp.exp(g_last) + _dot(kt, v16[hh])
        for hh in heads:
            lanes = slice(hh * LANES, (hh + 1) * LANES)
            o = r[hh][C:] + _dot(a_ref[hh, rows, :], v16[hh])
            o_ref[rows, lanes] = (_rms(o, gain_ref[...]) * z_ref[rows, lanes].astype(F32)).astype(o_ref.dtype)
        return carry

    lax.fori_loop(0, n_chunks, recur, 0)


def _delta(qkvz, gb, gain, *, n_heads, seq_len):
    M = qkvz.shape[0]
    W = n_heads * LANES
    B = M // seq_len
    nh = min(n_heads, 8)
    ncb = LANES // nh
    tb = ncb * CHUNK
    hw = nh * LANES
    nb = W // hw
    nt = seq_len // tb
    assert LANES % nh == 0 and n_heads % nh == 0 and seq_len % tb == 0 and ncb % SUBLANES == 0
    tok = lambda off: pl.BlockSpec((tb, hw), lambda b, h, t: (b * nt + t, h + off))
    return pl.pallas_call(
        functools.partial(_delta_kernel, n_heads=nh, n_chunks=ncb),
        grid=(B, nb, nt),
        in_specs=[tok(0), tok(nb), tok(2 * nb), tok(3 * nb),
                  pl.BlockSpec((nh, 1, ncb, CHUNK), lambda b, h, t: (h, b, t, 0)),
                  pl.BlockSpec((nh, 1, ncb, CHUNK), lambda b, h, t: (h + nb, b, t, 0)),
                  pl.BlockSpec((1, LANES), lambda b, h, t: (0, 0))],
        out_specs=tok(0),
        out_shape=jax.ShapeDtypeStruct((M, W), BF16),
        scratch_shapes=[pltpu.VMEM((nh, ncb, CHUNK), F32),
                        pltpu.VMEM((nh, LANES, LANES), F32),
                        pltpu.VMEM((LANES * A_PITCH, LANES), F32),
                        pltpu.VMEM((CHUNK, CHUNK, LANES), F32),
                        pltpu.VMEM((CHUNK, CHUNK, LANES), F32),
                        pltpu.VMEM((nh, tb, LANES), F32),
                        pltpu.VMEM((nh, 2 * tb, LANES), BF16),
                        pltpu.VMEM((nh, ncb * LANES, CHUNK), BF16),
                        pltpu.VMEM((nh, tb, CHUNK), BF16)],
        compiler_params=_params("parallel", "parallel", "arbitrary"),
        name="delta_rule",
    )(qkvz, qkvz, qkvz, qkvz, gb, gb, gain.reshape(1, LANES))


def _merge_kernel(u_ref, yc_ref, yd_ref, wgc_ref, wgd_ref, wc_ref, wdn_ref, o_ref):
    u = u_ref[...]
    gc = jax.nn.sigmoid(_dot_nt(u, wgc_ref[...]))
    gd = jax.nn.sigmoid(_dot_nt(u, wgd_ref[...]))
    merged = gc * _dot(yc_ref[...], wc_ref[...]) + gd * _dot(yd_ref[...], wdn_ref[...])
    o_ref[...] = merged.astype(o_ref.dtype)


def _out_proj_kernel(m_ref, w_ref, h_ref, o_ref):
    o_ref[...] = h_ref[...] + _dot(m_ref[...], w_ref[...])


def _merge(u, y_conv, y_dn, h, wg_t, wc, wdn, wo):
    M, D = h.shape
    Wc = y_conv.shape[1]
    Wd = y_dn.shape[1]
    tm = _tile(M, 1024, SUBLANES)
    tn = _tile(D, MXU_COLS)
    nn = D // tn
    row = lambda i, j: (i, 0)
    colw = lambda i, j: (0, j)
    merged = pl.pallas_call(
        _merge_kernel,
        grid=(M // tm, nn),
        in_specs=[pl.BlockSpec((tm, D), row), _single((tm, Wc), row), _single((tm, Wd), row),
                  pl.BlockSpec((tn, D), lambda i, j: (j, 0)), pl.BlockSpec((tn, D), lambda i, j: (j + nn, 0)),
                  pl.BlockSpec((Wc, tn), colw), pl.BlockSpec((Wd, tn), colw)],
        out_specs=pl.BlockSpec((tm, tn), lambda i, j: (i, j)),
        out_shape=jax.ShapeDtypeStruct((M, D), BF16),
        compiler_params=_params("parallel", "arbitrary"),
        name="merge_gate",
    )(u, y_conv, y_dn, wg_t, wg_t, wc, wdn)
    to = _tile(D, 1024)
    return pl.pallas_call(
        _out_proj_kernel,
        grid=(M // tm, D // to),
        in_specs=[pl.BlockSpec((tm, D), row), pl.BlockSpec((D, to), colw),
                  pl.BlockSpec((tm, to), lambda i, j: (i, j))],
        out_specs=pl.BlockSpec((tm, to), lambda i, j: (i, j)),
        out_shape=jax.ShapeDtypeStruct((M, D), F32),
        compiler_params=_params("parallel", "arbitrary"),
        name="out_proj",
    )(merged, wo, h)


def kernel(x, ffn1_norm, ffn1_w_gate, ffn1_w_up, ffn1_w_down, mix_norm, w_in, conv_mixer_w, dn_conv_w, dn_a_log, dn_dt_bias, dn_out_norm, w_conv_branch, w_dn_branch, w_out, ffn2_norm, ffn2_w_gate, ffn2_w_up, ffn2_w_down, final_norm):
    B, T, D = x.shape
    M = B * T
    depth = ffn1_norm.shape[0]
    Wc = conv_mixer_w.shape[1]
    Wd = dn_conv_w.shape[1] // 3
    H = dn_a_log.shape[1]
    assert Wd == H * LANES and dn_out_norm.shape[1] == LANES and T % CHUNK == 0
    c_q = 3 * Wc
    c_z = c_q + 3 * Wd
    c_a = c_z + Wd
    c_gc = c_a + 2 * H
    c_gd = c_gc + D
    assert w_in.shape[2] == c_gd + D
    bf = lambda w: w.astype(BF16)

    h = x.reshape(M, D)
    for l in range(depth):
        last = l == depth - 1
        *head, wg1, wu1, wd1 = _ffn_head(h, ffn1_norm[l], ffn1_w_gate[l], ffn1_w_up[l], ffn1_w_down[l], mix_norm[l])
        w_in_t = w_in[l].T
        h, u, w_in16, w_gates16 = _ffn(h, ffn1_norm[l], wg1, wu1, wd1, mix_norm[l], final=False, head=head,
                                       casts=((w_in_t, 0, c_gc), (w_in_t, c_gc, 2 * D)))

        gb = _gates(u, w_in16[c_a:c_gc], dn_a_log[l], dn_dt_bias[l])
        y_conv, wc16, wdn16, wo16 = _proj_gate(u, w_in16, conv_mixer_w[l].T, width=Wc, seq_len=T,
                                               casts=(w_conv_branch[l], w_dn_branch[l], w_out[l]))
        taps = dn_conv_w.shape[2]
        identity = jnp.zeros((taps, Wd), F32).at[taps - 1].set(1.0)
        qkvz, wg2, wu2, wd2 = _proj_conv(u, w_in16, jnp.concatenate([dn_conv_w[l].T, identity], axis=1),
                                         row0=c_q, n_norm_cols=2 * Wd, seq_len=T,
                                         casts=(ffn2_w_gate[l], ffn2_w_up[l], ffn2_w_down[l]))
        y_dn = _delta(qkvz, gb.reshape(2 * H, B, T // CHUNK, CHUNK), dn_out_norm[l], n_heads=H, seq_len=T)
        h = _merge(u, y_conv, y_dn, h, w_gates16, wc16, wdn16, wo16)

        if last:
            h = _ffn(h, ffn2_norm[l], wg2, wu2, wd2, final_norm, final=True)
        else:
            h = _ffn(h, ffn2_norm[l], wg2, wu2, wd2, final_norm, final=False)[0]
    return h.reshape(B, T, D)
```

```python
import functools

import jax
import jax.numpy as jnp
from jax import lax
from jax.experimental import pallas as pl
from jax.experimental.pallas import tpu as pltpu

F32 = jnp.float32
BF16 = jnp.bfloat16
EPS = 1e-6
L2_EPS = 1e-6
CHUNK = 64
LANES = 128
SUBLANES = 8
MXU_COLS = 256
ROW_CHUNK = 32
FFN_ROWS = 512
FFN_TILES = (512, 256)
V7X_VMEM_LIMIT_BYTES = 56 * 1024 * 1024
HIGHEST = lax.Precision.HIGHEST


def _params(*sem):
    return pltpu.CompilerParams(dimension_semantics=sem, vmem_limit_bytes=V7X_VMEM_LIMIT_BYTES)


def _tile(n, pref, mult=LANES):
    if n <= pref:
        return n
    t = (pref // mult) * mult
    while t > 0 and n % t:
        t -= mult
    assert t > 0, (n, pref, mult)
    return t


def _single(block, index_map):
    return pl.BlockSpec(block, index_map, pipeline_mode=pl.Buffered(1))


def _rms(h, gain):
    return h * lax.rsqrt(jnp.mean(h * h, axis=-1, keepdims=True) + EPS) * gain


def _for_rows(n_rows, fn):
    def body(r, carry):
        fn(pl.ds(pl.multiple_of(r * ROW_CHUNK, ROW_CHUNK), ROW_CHUNK))
        return carry
    lax.fori_loop(0, n_rows // ROW_CHUNK, body, 0)


def _dot(a, b, **kw):
    return jnp.dot(a, b, preferred_element_type=F32, **kw)


def _dot_nt(a, b, **kw):
    return lax.dot_general(a, b, (((1,), (1,)), ((), ())), preferred_element_type=F32, **kw)


def _dot_tn(a, b, **kw):
    return lax.dot_general(a, b, (((0,), (0,)), ((), ())), preferred_element_type=F32, **kw)


def _spread(shape, grid, row0=0):
    R, Cn = shape
    gi, gj = (1,) * (2 - len(grid)) + tuple(grid)
    best = None
    for shift in range(12):
        bc = -(-Cn // (LANES << shift)) * LANES
        ncol = -(-Cn // bc)
        if ncol != 1 << shift:
            continue
        nr_max = (gi * gj) // ncol
        if nr_max == 0:
            break
        br = -(-R // (nr_max * 2 * SUBLANES)) * 2 * SUBLANES
        nr = -(-R // br)
        if row0 % br:
            continue
        key = (R % br == 0 and Cn % bc == 0 and 2 * nr * ncol >= gi * gj, nr * ncol)
        if best is None or key > best[0]:
            best = (key, br, bc, nr, shift)
    assert best is not None, f"no (16a, 128b) blocking of {shape} from row {row0} fits {gi * gj} steps"
    _, br, bc, nr, shift = best

    def index_map(*ids, offset=0):
        step = ids[0] if len(ids) == 1 else ids[0] * gj + ids[1]
        s = jnp.minimum(step, (nr << shift) - 1)
        return (s >> shift) + offset, s & ((1 << shift) - 1)
    return (br, bc), index_map, functools.partial(index_map, offset=row0 // br)


def _cast_specs(casts, grid):
    arrays, in_specs, out_specs, shapes = [], [], [], []
    for c in casts:
        a, row0, n_rows = c if isinstance(c, tuple) else (c, 0, c.shape[0])
        block, out_map, in_map = _spread((n_rows, a.shape[1]), grid, row0)
        arrays.append(a)
        in_specs.append(pl.BlockSpec(block, in_map))
        out_specs.append(pl.BlockSpec(block, out_map))
        shapes.append(jax.ShapeDtypeStruct((n_rows, a.shape[1]), BF16))
    return arrays, in_specs, out_specs, shapes


def _do_casts(src_refs, dst_refs):
    for src, dst in zip(src_refs, dst_refs):
        dst[...] = src[...].astype(BF16)


def _swiglu_tile(xn, wg, wu, wd, o_ref):
    n_cols = wg.shape[1]
    parts = [(c, min(c + MXU_COLS, n_cols)) for c in range(0, n_cols, MXU_COLS)]
    gate_up = [(_dot(xn, wg[:, a:b]), _dot(xn, wu[:, a:b])) for a, b in parts]
    for (a, b), (gate, up) in zip(parts, gate_up):
        act = (gate * jax.nn.sigmoid(gate) * (0.5 * up)).astype(BF16)
        o_ref[...] += _dot(act, wd[a:b, :])


def _ffn_kernel(h_ref, g_ref, wg_ref, wu_ref, wd_ref, ng_ref, *refs, final, n_casts, last_cols, has_head):
    cast_in, refs = refs[:n_casts], refs[n_casts:]
    if has_head:
        (head_o, head_u), refs = refs[:2], refs[2:]
    if final:
        o_ref, xn_ref = refs
        cast_out = ()
    elif has_head:
        o_ref, u_ref, *cast_out, xn_ref, sem = refs
    else:
        o_ref, u_ref, *cast_out, xn_ref = refs
    i, j = pl.program_id(0), pl.program_id(1)
    last = pl.num_programs(1) - 1
    active = (i > 0) if has_head else True

    if has_head:
        @pl.when((i == 0) & (j == 0))
        def _():
            copies = [pltpu.make_async_copy(head_o, o_ref, sem.at[0]), pltpu.make_async_copy(head_u, u_ref, sem.at[1])]
            for cp in copies:
                cp.start()
            for cp in copies:
                cp.wait()

        pl.when(i == 0)(lambda: _do_casts(cast_in, cast_out))

    @pl.when(active & (j == 0))
    def _():
        def init(rows):
            h = h_ref[rows, :]
            xn_ref[rows, :] = _rms(h, g_ref[...]).astype(BF16)
            o_ref[rows, :] = h
        _for_rows(h_ref.shape[0], init)

    def ff_tile(n_cols):
        _swiglu_tile(xn_ref[...], wg_ref.at[:, :n_cols], wu_ref.at[:, :n_cols], wd_ref.at[:n_cols, :], o_ref)
        _do_casts(cast_in, cast_out)

    if last_cols == wg_ref.shape[1]:
        pl.when(active)(lambda: ff_tile(last_cols))
    else:
        pl.when(active & (j < last))(lambda: ff_tile(wg_ref.shape[1]))
        pl.when(active & (j == last))(lambda: ff_tile(last_cols))

    @pl.when(active & (j == last))
    def _():
        def fin(rows):
            y = _rms(o_ref[rows, :], ng_ref[...])
            if final:
                o_ref[rows, :] = y
            else:
                u_ref[rows, :] = y.astype(BF16)
        _for_rows(h_ref.shape[0], fin)


def _ffn(h, gain, wg, wu, wd, next_gain, *, final, casts=(), head=None):
    M, D = h.shape
    F = wg.shape[1]
    tm = _tile(M, FFN_ROWS, ROW_CHUNK)
    has_head = head is not None

    def vmem_bytes(tf):
        steps = (M // tm) * -(-F // tf)
        rows = tm * D * (4 + 4 + 2 + (0 if final else 2))
        elems = [c[2] * c[0].shape[1] if isinstance(c, tuple) else c.size for c in casts]
        side = 0 if final else sum(2 * 6 * -(-n // steps) for n in elems)
        return rows + 2 * 3 * D * tf * 2 + side + 3 * tm * tf * 4

    tf = next((t for t in FFN_TILES if vmem_bytes(min(F, t)) <= V7X_VMEM_LIMIT_BYTES), FFN_TILES[-1])
    tf = min(F, tf)
    nj = -(-F // tf)
    last_cols = F - (nj - 1) * tf
    assert last_cols % LANES == 0
    grid = (M // tm, nj)
    row = lambda i, j: (i, 0)
    col = (lambda i, j: jnp.where(i == 0, 0, j)) if has_head else (lambda i, j: j)
    out_shape = [jax.ShapeDtypeStruct((M, D), F32)]
    out_specs = [_single((tm, D), row)]
    cast_specs, head_specs, scratch = [], [], [pltpu.VMEM((tm, D), BF16)]
    if not final:
        out_shape.append(jax.ShapeDtypeStruct((M, D), BF16))
        out_specs.append(_single((tm, D), row))
        casts, cast_specs, cast_out_specs, cast_shapes = _cast_specs(casts, grid)
        out_specs += cast_out_specs
        out_shape += cast_shapes
    if has_head:
        head_specs = [pl.BlockSpec(memory_space=pl.ANY)] * 2
        scratch.append(pltpu.SemaphoreType.DMA((2,)))
    res = pl.pallas_call(
        functools.partial(_ffn_kernel, final=final, n_casts=len(cast_specs), last_cols=last_cols, has_head=has_head),
        grid=grid,
        in_specs=[
            _single((tm, D), (lambda i, j: (jnp.maximum(i, 1), 0)) if has_head else row),
            pl.BlockSpec((1, D), lambda i, j: (0, 0)),
            pl.BlockSpec((D, tf), lambda i, j: (0, col(i, j))),
            pl.BlockSpec((D, tf), lambda i, j: (0, col(i, j))),
            pl.BlockSpec((tf, D), lambda i, j: (col(i, j), 0)),
            pl.BlockSpec((1, D), lambda i, j: (0, 0)),
        ] + cast_specs + head_specs,
        out_specs=out_specs,
        out_shape=out_shape,
        scratch_shapes=scratch,
        compiler_params=_params("parallel", "arbitrary"),
        name="ffn_final" if final else "ffn",
    )(h, gain.reshape(1, D), wg, wu, wd, next_gain.reshape(1, D), *casts, *(head or ()))
    return res[0] if final else res


def _ffn_head_kernel(h_hbm, g_ref, wg_ref, wu_ref, wd_ref, ng_ref, o_ref, u_ref, wg16_ref, wu16_ref, wd16_ref,
                     xn_ref, sem):
    j = pl.program_id(0)
    tm = o_ref.shape[0]

    @pl.when(j == 0)
    def _():
        cp = pltpu.make_async_copy(h_hbm.at[pl.ds(0, tm), :], o_ref, sem)
        cp.start()
        cp.wait()

        def init(rows):
            xn_ref[rows, :] = _rms(o_ref[rows, :], g_ref[...]).astype(BF16)
        _for_rows(tm, init)

    wg16_ref[...] = wg_ref[...].astype(BF16)
    wu16_ref[...] = wu_ref[...].astype(BF16)
    wd16_ref[...] = wd_ref[...].astype(BF16)
    _swiglu_tile(xn_ref[...], wg16_ref, wu16_ref, wd16_ref, o_ref)

    @pl.when(j == pl.num_programs(0) - 1)
    def _():
        def fin(rows):
            u_ref[rows, :] = _rms(o_ref[rows, :], ng_ref[...]).astype(BF16)
        _for_rows(tm, fin)


def _ffn_head(h, gain, wg, wu, wd, next_gain):
    M, D = h.shape
    F = wg.shape[1]
    tm = _tile(M, FFN_ROWS, ROW_CHUNK)
    tf = _tile(F, MXU_COLS)
    const = lambda j: (0, 0)
    wcol = pl.BlockSpec((D, tf), lambda j: (0, j))
    wrow = pl.BlockSpec((tf, D), lambda j: (j, 0))
    return pl.pallas_call(
        _ffn_head_kernel,
        grid=(F // tf,),
        in_specs=[pl.BlockSpec(memory_space=pl.ANY), pl.BlockSpec((1, D), const), wcol, wcol, wrow,
                  pl.BlockSpec((1, D), const)],
        out_specs=[_single((tm, D), const), _single((tm, D), const), wcol, wcol, wrow],
        out_shape=[jax.ShapeDtypeStruct((tm, D), F32), jax.ShapeDtypeStruct((tm, D), BF16),
                   jax.ShapeDtypeStruct(wg.shape, BF16), jax.ShapeDtypeStruct(wu.shape, BF16),
                   jax.ShapeDtypeStruct(wd.shape, BF16)],
        scratch_shapes=[pltpu.VMEM((tm, D), BF16), pltpu.SemaphoreType.DMA(())],
        compiler_params=_params("arbitrary"),
        name="ffn_head",
    )(h, gain.reshape(1, D), wg, wu, wd, next_gain.reshape(1, D))


CONV_ROWS = 64


def _conv_piece(z_ref, w_ref, K, r0, lanes):
    zz = z_ref[pl.ds(r0, CONV_ROWS + SUBLANES), lanes]
    acc = zz[SUBLANES:, :] * w_ref[K - 1:K, lanes]
    for s in range(1, K):
        acc = acc + pltpu.roll(zz, s, axis=0)[SUBLANES:, :] * w_ref[K - 1 - s:K - s, lanes]
    return acc


def _load_halo(z_ref, halo_ref, j, first):
    @pl.when(first)
    def _():
        z_ref[0:SUBLANES, :] = jnp.zeros((SUBLANES, z_ref.shape[1]), F32)

    @pl.when(jnp.logical_not(first))
    def _():
        z_ref[0:SUBLANES, :] = halo_ref[j]


def _proj_gate_kernel(u_ref, wb_ref, wc_ref, wx_ref, cw_ref, *refs, tiles_per_seq, K, n_casts):
    cast_in, o_ref, cast_out, (z_ref, halo_ref) = refs[:n_casts], refs[n_casts], refs[n_casts + 1:-2], refs[-2:]
    i, j = pl.program_id(0), pl.program_id(1)
    tm, tn = o_ref.shape
    _load_halo(z_ref, halo_ref, j, (i % tiles_per_seq) == 0)
    _do_casts(cast_in, cast_out)
    u = u_ref[...]
    z_ref[SUBLANES:, :] = _dot_nt(u, wc_ref[...]) * _dot_nt(u, wx_ref[...])
    halo_ref[j] = z_ref[tm:tm + SUBLANES, :]
    b = _dot_nt(u, wb_ref[...])
    for r0 in range(0, tm, CONV_ROWS):
        for c0 in range(0, tn, LANES):
            lanes = slice(c0, c0 + LANES)
            y = b[r0:r0 + CONV_ROWS, lanes] * _conv_piece(z_ref, cw_ref, K, r0, lanes)
            o_ref[r0:r0 + CONV_ROWS, lanes] = y.astype(o_ref.dtype)


def _proj_gate(u, w_t, conv_w, *, width, seq_len, casts=()):
    M, D = u.shape
    K = conv_w.shape[0]
    tm = _tile(seq_len, 1024, CONV_ROWS)
    tn = _tile(width, MXU_COLS)
    nj = width // tn
    grid = (M // tm, nj)
    wspec = lambda off: pl.BlockSpec((tn, D), lambda i, j: (j + off, 0))
    casts, cast_specs, cast_out_specs, cast_shapes = _cast_specs(casts, grid)
    return pl.pallas_call(
        functools.partial(_proj_gate_kernel, tiles_per_seq=seq_len // tm, K=K, n_casts=len(casts)),
        grid=grid,
        in_specs=[pl.BlockSpec((tm, D), lambda i, j: (i, 0)), wspec(0), wspec(nj), wspec(2 * nj),
                  pl.BlockSpec((K, tn), lambda i, j: (0, j))] + cast_specs,
        out_specs=[pl.BlockSpec((tm, tn), lambda i, j: (i, j))] + cast_out_specs,
        out_shape=[jax.ShapeDtypeStruct((M, width), BF16)] + cast_shapes,
        scratch_shapes=[pltpu.VMEM((tm + SUBLANES, tn), F32), pltpu.VMEM((nj, SUBLANES, tn), F32)],
        compiler_params=_params("arbitrary", "arbitrary"),
        name="proj_gate",
    )(u, w_t, w_t, w_t, conv_w, *casts)


def _proj_conv_kernel(u_ref, w_ref, cw_ref, *refs, nj, tiles_per_seq, K, n_norm_tiles, n_casts):
    cast_in, o_ref, cast_out, (z_ref, halo_ref) = refs[:n_casts], refs[n_casts], refs[n_casts + 1:-2], refs[-2:]
    s = pl.program_id(0)
    tm, tn = o_ref.shape

    @pl.when(s == 0)
    def _():
        z_ref[...] = jnp.zeros_like(z_ref)

    t = jnp.maximum(s - 1, 0)
    i, j = t // nj, t % nj
    _load_halo(z_ref, halo_ref, j, (i % tiles_per_seq) == 0)
    normed = j < n_norm_tiles
    for r0 in range(0, tm, CONV_ROWS):
        for c0 in range(0, tn, LANES):
            lanes = slice(c0, c0 + LANES)
            acc = _conv_piece(z_ref, cw_ref, K, r0, lanes)
            y = acc * jax.nn.sigmoid(acc)
            ss = jnp.sum(y * y, axis=-1, keepdims=True)
            y = y * jnp.where(normed, lax.rsqrt(ss + L2_EPS), 1.0)
            o_ref[r0:r0 + CONV_ROWS, lanes] = y.astype(o_ref.dtype)
    halo_ref[j] = z_ref[tm:tm + SUBLANES, :]
    z_ref[SUBLANES:, :] = _dot_nt(u_ref[...], w_ref[...])
    _do_casts(cast_in, cast_out)


def _proj_conv(u, w_t, conv_w, *, row0, n_norm_cols, seq_len, casts=()):
    M, D = u.shape
    K, N = conv_w.shape
    tm = _tile(seq_len, 1024, CONV_ROWS)
    tn = _tile(N, 512)
    nj = N // tn
    steps = (M // tm) * nj
    assert row0 % tn == 0 and n_norm_cols % tn == 0
    off = row0 // tn
    cur = lambda s: jnp.minimum(s, steps - 1)
    prev = lambda s: jnp.maximum(s - 1, 0)
    casts, cast_specs, cast_out_specs, cast_shapes = _cast_specs(casts, (steps + 1,))
    return pl.pallas_call(
        functools.partial(_proj_conv_kernel, nj=nj, tiles_per_seq=seq_len // tm, K=K, n_norm_tiles=n_norm_cols // tn,
                          n_casts=len(casts)),
        grid=(steps + 1,),
        in_specs=[pl.BlockSpec((tm, D), lambda s: (cur(s) // nj, 0)),
                  pl.BlockSpec((tn, D), lambda s: (cur(s) % nj + off, 0)),
                  pl.BlockSpec((K, tn), lambda s: (0, prev(s) % nj))] + cast_specs,
        out_specs=[pl.BlockSpec((tm, tn), lambda s: (prev(s) // nj, prev(s) % nj))] + cast_out_specs,
        out_shape=[jax.ShapeDtypeStruct((M, N), BF16)] + cast_shapes,
        scratch_shapes=[pltpu.VMEM((tm + SUBLANES, tn), F32), pltpu.VMEM((nj, SUBLANES, tn), F32)],
        compiler_params=_params("arbitrary"),
        name="proj_conv",
    )(u, w_t, conv_w, *casts)


def _gates_kernel(u_ref, w_ref, alog_ref, bias_ref, o_ref, *, n_heads):
    x = _dot_nt(w_ref[...], u_ref[...])
    s = x + bias_ref[...]
    softplus = jnp.maximum(s, 0.0) + jnp.log1p(jnp.exp(-jnp.abs(s)))
    g = -jnp.exp(alog_ref[...]) * softplus
    row = lax.broadcasted_iota(jnp.int32, x.shape, 0)
    o_ref[...] = jnp.where(row < n_heads, g, jax.nn.sigmoid(x))


def _gates(u, w_ab_t, a_log, dt_bias):
    M, D = u.shape
    H = a_log.shape[0]
    tm = _tile(M, 512)
    pad = jnp.zeros((H,), F32)
    col = lambda v: jnp.concatenate([v.astype(F32), pad]).reshape(2 * H, 1)
    return pl.pallas_call(
        functools.partial(_gates_kernel, n_heads=H),
        grid=(M // tm,),
        in_specs=[pl.BlockSpec((tm, D), lambda i: (i, 0)),
                  pl.BlockSpec((2 * H, D), lambda i: (0, 0)),
                  pl.BlockSpec((2 * H, 1), lambda i: (0, 0)),
                  pl.BlockSpec((2 * H, 1), lambda i: (0, 0))],
        out_specs=pl.BlockSpec((2 * H, tm), lambda i: (0, i)),
        out_shape=jax.ShapeDtypeStruct((2 * H, M), F32),
        compiler_params=_params("parallel"),
        name="gates",
    )(u, w_ab_t, col(a_log), col(dt_bias))


A_PITCH = CHUNK + SUBLANES


def _delta_kernel(q_ref, k_ref, v_ref, z_ref, g_ref, b_ref, gain_ref, o_ref,
                  gc_ref, s_ref, am_ref, at_ref, tm_ref, u_ref, wq_ref, kt_ref, a_ref, *, n_heads, n_chunks):
    C = CHUNK
    nc = n_heads * n_chunks
    scale = LANES ** -0.5
    row = lax.broadcasted_iota(jnp.int32, (C, C), 0)
    col = lax.broadcasted_iota(jnp.int32, (C, C), 1)
    eye = row == col
    causal = row >= col
    strict = row > col
    zeros_cc = jnp.zeros((C, C), F32)
    lane_row = lax.broadcasted_iota(jnp.int32, (LANES, LANES), 0)
    lane_col = lax.broadcasted_iota(jnp.int32, (LANES, LANES), 1)
    eye_lanes = jnp.where(lane_row == lane_col, 1.0, 0.0).astype(BF16)

    @pl.when(pl.program_id(2) == 0)
    def _():
        s_ref[...] = jnp.zeros_like(s_ref)

    prefix = jnp.where(row <= col, 1.0, 0.0)
    for hh in range(n_heads):
        gc_ref[hh] = _dot(g_ref[hh, 0], prefix, precision=HIGHEST)

    def to_col(r):
        return jnp.sum(jnp.where(eye, jnp.broadcast_to(r, (C, C)), 0.0), axis=1, keepdims=True)

    def chunk_slices(hh, c):
        return pl.ds(pl.multiple_of(c * C, C), C), slice(hh * LANES, (hh + 1) * LANES)

    def am_rows(hh, c):
        return pl.ds(pl.multiple_of((hh * n_chunks + c) * A_PITCH, SUBLANES), C)

    def intra(c, carry):
        for hh in range(n_heads):
            rows, lanes = chunk_slices(hh, c)
            k16 = k_ref[rows, lanes]
            k = k16.astype(F32)
            q = q_ref[rows, lanes].astype(F32) * scale
            g_row = gc_ref[hh, pl.ds(c, 1), :]
            g_col = to_col(g_row)
            b_col = to_col(b_ref[hh, 0, pl.ds(c, 1), :])
            decay = jnp.where(causal, jnp.exp(jnp.where(causal, g_col - g_row, 0.0)), 0.0)
            prod = _dot_nt(jnp.concatenate([k * b_col, q], axis=0).astype(BF16), k16)
            a = jnp.where(strict, prod[:C] * decay, 0.0)
            am_ref[am_rows(hh, c), :] = jnp.concatenate([a, zeros_cc], axis=1)
            a_ref[hh, rows, :] = (prod[C:] * decay).astype(BF16)
        return carry

    lax.fori_loop(0, n_chunks, intra, 0)

    for i in range(C):
        at_ref[i] = am_ref[pl.ds(i, nc, stride=A_PITCH), :].T[:C, :]

    zero_blk = jnp.zeros((SUBLANES, nc), F32)
    for i in range(C):
        nb = -(-i // SUBLANES)
        acc = [-at_ref[i, b * SUBLANES:(b + 1) * SUBLANES, :] for b in range(nb)]
        for j in range(1, i):
            a_ij = jnp.broadcast_to(at_ref[i, j:j + 1, :], (SUBLANES, nc))
            for b in range(-(-j // SUBLANES)):
                acc[b] = acc[b] - a_ij * tm_ref[j, b * SUBLANES:(b + 1) * SUBLANES, :]
        for b in range(C // SUBLANES):
            tm_ref[i, b * SUBLANES:(b + 1) * SUBLANES, :] = acc[b] if b < nb else zero_blk

    zeros_pad = jnp.zeros((LANES - C, nc), F32)
    for i in range(C):
        am_ref[pl.ds(i, nc, stride=A_PITCH), :] = jnp.concatenate([tm_ref[i], zeros_pad], axis=0).T

    def solve(c, carry):
        for hh in range(n_heads):
            rows, lanes = chunk_slices(hh, c)
            k = k_ref[rows, lanes].astype(F32)
            v = v_ref[rows, lanes].astype(F32)
            q = q_ref[rows, lanes].astype(F32) * scale
            g_row = gc_ref[hh, pl.ds(c, 1), :]
            g_col = to_col(g_row)
            b_col = to_col(b_ref[hh, 0, pl.ds(c, 1), :])
            e_col = jnp.exp(g_col)
            rhs = jnp.concatenate([v * b_col, k * (b_col * e_col)], axis=1)
            tm = am_ref[am_rows(hh, c), :][:, :C]
            sol = rhs + _dot(tm.astype(BF16), rhs.astype(BF16))
            u_ref[hh, rows, :] = sol[:, :LANES]
            wq_ref[hh, pl.ds(pl.multiple_of(c * 2 * C, 2 * C), C), :] = sol[:, LANES:].astype(BF16)
            wq_ref[hh, pl.ds(pl.multiple_of(c * 2 * C + C, C), C), :] = (q * e_col).astype(BF16)
            k_dec = k * jnp.exp(g_row[:, C - 1:C] - g_col)
            k_dec_t = _dot_nt(eye_lanes, k_dec.astype(BF16))
            kt_ref[hh, pl.ds(pl.multiple_of(c * LANES, LANES), LANES), :] = k_dec_t.astype(BF16)
        return carry

    lax.fori_loop(0, n_chunks, solve, 0)

    def recur(c, carry):
        heads = range(n_heads)
        rows = pl.ds(pl.multiple_of(c * C, C), C)
        S = [s_ref[hh] for hh in heads]
        r = [_dot(wq_ref[hh, pl.ds(pl.multiple_of(c * 2 * C, 2 * C), 2 * C), :], S[hh].astype(BF16))
             for hh in heads]
        v16 = [(u_ref[hh, rows, :] - r[hh][:C]).astype(BF16) for hh in heads]
        for hh in heads:
            g_last = gc_ref[hh, pl.ds(c, 1), :][:, C - 1:C]
            kt = kt_ref[hh, pl.ds(pl.multiple_of(c * LANES, LANES), LANES), :]
            s_ref[hh] = S[hh] * jnp.exp(g_last) + _dot(kt, v16[hh])
        for hh in heads:
            lanes = slice(hh * LANES, (hh + 1) * LANES)
            o = r[hh][C:] + _dot(a_ref[hh, rows, :], v16[hh])
            o_ref[rows, lanes] = (_rms(o, gain_ref[...]) * z_ref[rows, lanes].astype(F32)).astype(o_ref.dtype)
        return carry

    lax.fori_loop(0, n_chunks, recur, 0)


def _delta(qkvz, gb, gain, *, n_heads, seq_len):
    M = qkvz.shape[0]
    W = n_heads * LANES
    B = M // seq_len
    nh = min(n_heads, 8)
    ncb = LANES // nh
    tb = ncb * CHUNK
    hw = nh * LANES
    nb = W // hw
    nt = seq_len // tb
    assert LANES % nh == 0 and n_heads % nh == 0 and seq_len % tb == 0 and ncb % SUBLANES == 0
    tok = lambda off: pl.BlockSpec((tb, hw), lambda b, h, t: (b * nt + t, h + off))
    return pl.pallas_call(
        functools.partial(_delta_kernel, n_heads=nh, n_chunks=ncb),
        grid=(B, nb, nt),
        in_specs=[tok(0), tok(nb), tok(2 * nb), tok(3 * nb),
                  pl.BlockSpec((nh, 1, ncb, CHUNK), lambda b, h, t: (h, b, t, 0)),
                  pl.BlockSpec((nh, 1, ncb, CHUNK), lambda b, h, t: (h + nb, b, t, 0)),
                  pl.BlockSpec((1, LANES), lambda b, h, t: (0, 0))],
        out_specs=tok(0),
        out_shape=jax.ShapeDtypeStruct((M, W), BF16),
        scratch_shapes=[pltpu.VMEM((nh, ncb, CHUNK), F32),
                        pltpu.VMEM((nh, LANES, LANES), F32),
                        pltpu.VMEM((LANES * A_PITCH, LANES), F32),
                        pltpu.VMEM((CHUNK, CHUNK, LANES), F32),
                        pltpu.VMEM((CHUNK, CHUNK, LANES), F32),
                        pltpu.VMEM((nh, tb, LANES), F32),
                        pltpu.VMEM((nh, 2 * tb, LANES), BF16),
                        pltpu.VMEM((nh, ncb * LANES, CHUNK), BF16),
                        pltpu.VMEM((nh, tb, CHUNK), BF16)],
        compiler_params=_params("parallel", "parallel", "arbitrary"),
        name="delta_rule",
    )(qkvz, qkvz, qkvz, qkvz, gb, gb, gain.reshape(1, LANES))


def _merge_kernel(u_ref, yc_ref, yd_ref, wgc_ref, wgd_ref, wc_ref, wdn_ref, o_ref):
    u = u_ref[...]
    gc = jax.nn.sigmoid(_dot_nt(u, wgc_ref[...]))
    gd = jax.nn.sigmoid(_dot_nt(u, wgd_ref[...]))
    merged = gc * _dot(yc_ref[...], wc_ref[...]) + gd * _dot(yd_ref[...], wdn_ref[...])
    o_ref[...] = merged.astype(o_ref.dtype)


def _out_proj_kernel(m_ref, w_ref, h_ref, o_ref):
    o_ref[...] = h_ref[...] + _dot(m_ref[...], w_ref[...])


def _merge(u, y_conv, y_dn, h, wg_t, wc, wdn, wo):
    M, D = h.shape
    Wc = y_conv.shape[1]
    Wd = y_dn.shape[1]
    tm = _tile(M, 1024, SUBLANES)
    tn = _tile(D, MXU_COLS)
    nn = D // tn
    row = lambda i, j: (i, 0)
    colw = lambda i, j: (0, j)
    merged = pl.pallas_call(
        _merge_kernel,
        grid=(M // tm, nn),
        in_specs=[pl.BlockSpec((tm, D), row), _single((tm, Wc), row), _single((tm, Wd), row),
                  pl.BlockSpec((tn, D), lambda i, j: (j, 0)), pl.BlockSpec((tn, D), lambda i, j: (j + nn, 0)),
                  pl.BlockSpec((Wc, tn), colw), pl.BlockSpec((Wd, tn), colw)],
        out_specs=pl.BlockSpec((tm, tn), lambda i, j: (i, j)),
        out_shape=jax.ShapeDtypeStruct((M, D), BF16),
        compiler_params=_params("parallel", "arbitrary"),
        name="merge_gate",
    )(u, y_conv, y_dn, wg_t, wg_t, wc, wdn)
    to = _tile(D, 1024)
    return pl.pallas_call(
        _out_proj_kernel,
        grid=(M // tm, D // to),
        in_specs=[pl.BlockSpec((tm, D), row), pl.BlockSpec((D, to), colw),
                  pl.BlockSpec((tm, to), lambda i, j: (i, j))],
        out_specs=pl.BlockSpec((tm, to), lambda i, j: (i, j)),
        out_shape=jax.ShapeDtypeStruct((M, D), F32),
        compiler_params=_params("parallel", "arbitrary"),
        name="out_proj",
    )(merged, wo, h)


def kernel(x, ffn1_norm, ffn1_w_gate, ffn1_w_up, ffn1_w_down, mix_norm, w_in, conv_mixer_w, dn_conv_w, dn_a_log, dn_dt_bias, dn_out_norm, w_conv_branch, w_dn_branch, w_out, ffn2_norm, ffn2_w_gate, ffn2_w_up, ffn2_w_down, final_norm):
    B, T, D = x.shape
    M = B * T
    depth = ffn1_norm.shape[0]
    Wc = conv_mixer_w.shape[1]
    Wd = dn_conv_w.shape[1] // 3
    H = dn_a_log.shape[1]
    assert Wd == H * LANES and dn_out_norm.shape[1] == LANES and T % CHUNK == 0
    c_q = 3 * Wc
    c_z = c_q + 3 * Wd
    c_a = c_z + Wd
    c_gc = c_a + 2 * H
    c_gd = c_gc + D
    assert w_in.shape[2] == c_gd + D
    bf = lambda w: w.astype(BF16)

    h = x.reshape(M, D)
    for l in range(depth):
        last = l == depth - 1
        *head, wg1, wu1, wd1 = _ffn_head(h, ffn1_norm[l], ffn1_w_gate[l], ffn1_w_up[l], ffn1_w_down[l], mix_norm[l])
        w_in_t = w_in[l].T
        h, u, w_in16, w_gates16 = _ffn(h, ffn1_norm[l], wg1, wu1, wd1, mix_norm[l], final=False, head=head,
                                       casts=((w_in_t, 0, c_gc), (w_in_t, c_gc, 2 * D)))

        gb = _gates(u, w_in16[c_a:c_gc], dn_a_log[l], dn_dt_bias[l])
        y_conv, wc16, wdn16, wo16 = _proj_gate(u, w_in16, conv_mixer_w[l].T, width=Wc, seq_len=T,
                                               casts=(w_conv_branch[l], w_dn_branch[l], w_out[l]))
        taps = dn_conv_w.shape[2]
        identity = jnp.zeros((taps, Wd), F32).at[taps - 1].set(1.0)
        qkvz, wg2, wu2, wd2 = _proj_conv(u, w_in16, jnp.concatenate([dn_conv_w[l].T, identity], axis=1),
                                         row0=c_q, n_norm_cols=2 * Wd, seq_len=T,
                                         casts=(ffn2_w_gate[l], ffn2_w_up[l], ffn2_w_down[l]))
        y_dn = _delta(qkvz, gb.reshape(2 * H, B, T // CHUNK, CHUNK), dn_out_norm[l], n_heads=H, seq_len=T)
        h = _merge(u, y_conv, y_dn, h, w_gates16, wc16, wdn16, wo16)

        if last:
            h = _ffn(h, ffn2_norm[l], wg2, wu2, wd2, final_norm, final=True)
        else:
            h = _ffn(h, ffn2_norm[l], wg2, wu2, wd2, final_norm, final=False)[0]
    return h.reshape(B, T, D)
```

```python
import functools

import jax
import jax.numpy as jnp
from jax import lax
from jax.experimental import pallas as pl
from jax.experimental.pallas import tpu as pltpu

F32 = jnp.float32
BF16 = jnp.bfloat16
EPS = 1e-6
L2_EPS = 1e-6
CHUNK = 64
LANES = 128
SUBLANES = 8
MXU_COLS = 256
ROW_CHUNK = 32
FFN_ROWS = 512
FFN_TILES = (512, 256)
V7X_VMEM_LIMIT_BYTES = 56 * 1024 * 1024
HIGHEST = lax.Precision.HIGHEST


def _params(*sem):
    return pltpu.CompilerParams(dimension_semantics=sem, vmem_limit_bytes=V7X_VMEM_LIMIT_BYTES)


def _tile(n, pref, mult=LANES):
    if n <= pref:
        return n
    t = (pref // mult) * mult
    while t > 0 and n % t:
        t -= mult
    assert t > 0, (n, pref, mult)
    return t


def _single(block, index_map):
    return pl.BlockSpec(block, index_map, pipeline_mode=pl.Buffered(1))


def _rms(h, gain):
    return h * lax.rsqrt(jnp.mean(h * h, axis=-1, keepdims=True) + EPS) * gain


def _for_rows(n_rows, fn):
    def body(r, carry):
        fn(pl.ds(pl.multiple_of(r * ROW_CHUNK, ROW_CHUNK), ROW_CHUNK))
        return carry
    lax.fori_loop(0, n_rows // ROW_CHUNK, body, 0)


def _dot(a, b, **kw):
    return jnp.dot(a, b, preferred_element_type=F32, **kw)


def _dot_nt(a, b, **kw):
    return lax.dot_general(a, b, (((1,), (1,)), ((), ())), preferred_element_type=F32, **kw)


def _dot_tn(a, b, **kw):
    return lax.dot_general(a, b, (((0,), (0,)), ((), ())), preferred_element_type=F32, **kw)


def _spread(shape, grid, row0=0):
    R, Cn = shape
    gi, gj = (1,) * (2 - len(grid)) + tuple(grid)
    best = None
    for shift in range(12):
        bc = -(-Cn // (LANES << shift)) * LANES
        ncol = -(-Cn // bc)
        if ncol != 1 << shift:
            continue
        nr_max = (gi * gj) // ncol
        if nr_max == 0:
            break
        br = -(-R // (nr_max * 2 * SUBLANES)) * 2 * SUBLANES
        nr = -(-R // br)
        if row0 % br:
            continue
        key = (R % br == 0 and Cn % bc == 0 and 2 * nr * ncol >= gi * gj, nr * ncol)
        if best is None or key > best[0]:
            best = (key, br, bc, nr, shift)
    assert best is not None, f"no (16a, 128b) blocking of {shape} from row {row0} fits {gi * gj} steps"
    _, br, bc, nr, shift = best

    def index_map(*ids, offset=0):
        step = ids[0] if len(ids) == 1 else ids[0] * gj + ids[1]
        s = jnp.minimum(step, (nr << shift) - 1)
        return (s >> shift) + offset, s & ((1 << shift) - 1)
    return (br, bc), index_map, functools.partial(index_map, offset=row0 // br)


def _cast_specs(casts, grid):
    arrays, in_specs, out_specs, shapes = [], [], [], []
    for c in casts:
        a, row0, n_rows = c if isinstance(c, tuple) else (c, 0, c.shape[0])
        block, out_map, in_map = _spread((n_rows, a.shape[1]), grid, row0)
        arrays.append(a)
        in_specs.append(pl.BlockSpec(block, in_map))
        out_specs.append(pl.BlockSpec(block, out_map))
        shapes.append(jax.ShapeDtypeStruct((n_rows, a.shape[1]), BF16))
    return arrays, in_specs, out_specs, shapes


def _do_casts(src_refs, dst_refs):
    for src, dst in zip(src_refs, dst_refs):
        dst[...] = src[...].astype(BF16)


def _swiglu_tile(xn, wg, wu, wd, o_ref):
    n_cols = wg.shape[1]
    parts = [(c, min(c + MXU_COLS, n_cols)) for c in range(0, n_cols, MXU_COLS)]
    gate_up = [(_dot(xn, wg[:, a:b]), _dot(xn, wu[:, a:b])) for a, b in parts]
    for (a, b), (gate, up) in zip(parts, gate_up):
        act = (gate * jax.nn.sigmoid(gate) * (0.5 * up)).astype(BF16)
        o_ref[...] += _dot(act, wd[a:b, :])


def _ffn_kernel(h_ref, g_ref, wg_ref, wu_ref, wd_ref, ng_ref, *refs, final, n_casts, last_cols, has_head):
    cast_in, refs = refs[:n_casts], refs[n_casts:]
    if has_head:
        (head_o, head_u), refs = refs[:2], refs[2:]
    if final:
        o_ref, xn_ref = refs
        cast_out = ()
    elif has_head:
        o_ref, u_ref, *cast_out, xn_ref, sem = refs
    else:
        o_ref, u_ref, *cast_out, xn_ref = refs
    i, j = pl.program_id(0), pl.program_id(1)
    last = pl.num_programs(1) - 1
    active = (i > 0) if has_head else True

    if has_head:
        @pl.when((i == 0) & (j == 0))
        def _():
            copies = [pltpu.make_async_copy(head_o, o_ref, sem.at[0]), pltpu.make_async_copy(head_u, u_ref, sem.at[1])]
            for cp in copies:
                cp.start()
            for cp in copies:
                cp.wait()

        pl.when(i == 0)(lambda: _do_casts(cast_in, cast_out))

    @pl.when(active & (j == 0))
    def _():
        def init(rows):
            h = h_ref[rows, :]
            xn_ref[rows, :] = _rms(h, g_ref[...]).astype(BF16)
            o_ref[rows, :] = h
        _for_rows(h_ref.shape[0], init)

    def ff_tile(n_cols):
        _swiglu_tile(xn_ref[...], wg_ref.at[:, :n_cols], wu_ref.at[:, :n_cols], wd_ref.at[:n_cols, :], o_ref)
        _do_casts(cast_in, cast_out)

    if last_cols == wg_ref.shape[1]:
        pl.when(active)(lambda: ff_tile(last_cols))
    else:
        pl.when(active & (j < last))(lambda: ff_tile(wg_ref.shape[1]))
        pl.when(active & (j == last))(lambda: ff_tile(last_cols))

    @pl.when(active & (j == last))
    def _():
        def fin(rows):
            y = _rms(o_ref[rows, :], ng_ref[...])
            if final:
                o_ref[rows, :] = y
            else:
                u_ref[rows, :] = y.astype(BF16)
        _for_rows(h_ref.shape[0], fin)


def _ffn(h, gain, wg, wu, wd, next_gain, *, final, casts=(), head=None):
    M, D = h.shape
    F = wg.shape[1]
    tm = _tile(M, FFN_ROWS, ROW_CHUNK)
    has_head = head is not None

    def vmem_bytes(tf):
        steps = (M // tm) * -(-F // tf)
        rows = tm * D * (4 + 4 + 2 + (0 if final else 2))
        elems = [c[2] * c[0].shape[1] if isinstance(c, tuple) else c.size for c in casts]
        side = 0 if final else sum(2 * 6 * -(-n // steps) for n in elems)
        return rows + 2 * 3 * D * tf * 2 + side + 3 * tm * tf * 4

    tf = next((t for t in FFN_TILES if vmem_bytes(min(F, t)) <= V7X_VMEM_LIMIT_BYTES), FFN_TILES[-1])
    tf = min(F, tf)
    nj = -(-F // tf)
    last_cols = F - (nj - 1) * tf
    assert last_cols % LANES == 0
    grid = (M // tm, nj)
    row = lambda i, j: (i, 0)
    col = (lambda i, j: jnp.where(i == 0, 0, j)) if has_head else (lambda i, j: j)
    out_shape = [jax.ShapeDtypeStruct((M, D), F32)]
    out_specs = [_single((tm, D), row)]
    cast_specs, head_specs, scratch = [], [], [pltpu.VMEM((tm, D), BF16)]
    if not final:
        out_shape.append(jax.ShapeDtypeStruct((M, D), BF16))
        out_specs.append(_single((tm, D), row))
        casts, cast_specs, cast_out_specs, cast_shapes = _cast_specs(casts, grid)
        out_specs += cast_out_specs
        out_shape += cast_shapes
    if has_head:
        head_specs = [pl.BlockSpec(memory_space=pl.ANY)] * 2
        scratch.append(pltpu.SemaphoreType.DMA((2,)))
    res = pl.pallas_call(
        functools.partial(_ffn_kernel, final=final, n_casts=len(cast_specs), last_cols=last_cols, has_head=has_head),
        grid=grid,
        in_specs=[
            _single((tm, D), (lambda i, j: (jnp.maximum(i, 1), 0)) if has_head else row),
            pl.BlockSpec((1, D), lambda i, j: (0, 0)),
            pl.BlockSpec((D, tf), lambda i, j: (0, col(i, j))),
            pl.BlockSpec((D, tf), lambda i, j: (0, col(i, j))),
            pl.BlockSpec((tf, D), lambda i, j: (col(i, j), 0)),
            pl.BlockSpec((1, D), lambda i, j: (0, 0)),
        ] + cast_specs + head_specs,
        out_specs=out_specs,
        out_shape=out_shape,
        scratch_shapes=scratch,
        compiler_params=_params("parallel", "arbitrary"),
        name="ffn_final" if final else "ffn",
    )(h, gain.reshape(1, D), wg, wu, wd, next_gain.reshape(1, D), *casts, *(head or ()))
    return res[0] if final else res


def _ffn_head_kernel(h_hbm, g_ref, wg_ref, wu_ref, wd_ref, ng_ref, o_ref, u_ref, wg16_ref, wu16_ref, wd16_ref,
                     xn_ref, sem):
    j = pl.program_id(0)
    tm = o_ref.shape[0]

    @pl.when(j == 0)
    def _():
        cp = pltpu.make_async_copy(h_hbm.at[pl.ds(0, tm), :], o_ref, sem)
        cp.start()
        cp.wait()

        def init(rows):
            xn_ref[rows, :] = _rms(o_ref[rows, :], g_ref[...]).astype(BF16)
        _for_rows(tm, init)

    wg16_ref[...] = wg_ref[...].astype(BF16)
    wu16_ref[...] = wu_ref[...].astype(BF16)
    wd16_ref[...] = wd_ref[...].astype(BF16)
    _swiglu_tile(xn_ref[...], wg16_ref, wu16_ref, wd16_ref, o_ref)

    @pl.when(j == pl.num_programs(0) - 1)
    def _():
        def fin(rows):
            u_ref[rows, :] = _rms(o_ref[rows, :], ng_ref[...]).astype(BF16)
        _for_rows(tm, fin)


def _ffn_head(h, gain, wg, wu, wd, next_gain):
    M, D = h.shape
    F = wg.shape[1]
    tm = _tile(M, FFN_ROWS, ROW_CHUNK)
    tf = _tile(F, MXU_COLS)
    const = lambda j: (0, 0)
    wcol = pl.BlockSpec((D, tf), lambda j: (0, j))
    wrow = pl.BlockSpec((tf, D), lambda j: (j, 0))
    return pl.pallas_call(
        _ffn_head_kernel,
        grid=(F // tf,),
        in_specs=[pl.BlockSpec(memory_space=pl.ANY), pl.BlockSpec((1, D), const), wcol, wcol, wrow,
                  pl.BlockSpec((1, D), const)],
        out_specs=[_single((tm, D), const), _single((tm, D), const), wcol, wcol, wrow],
        out_shape=[jax.ShapeDtypeStruct((tm, D), F32), jax.ShapeDtypeStruct((tm, D), BF16),
                   jax.ShapeDtypeStruct(wg.shape, BF16), jax.ShapeDtypeStruct(wu.shape, BF16),
                   jax.ShapeDtypeStruct(wd.shape, BF16)],
        scratch_shapes=[pltpu.VMEM((tm, D), BF16), pltpu.SemaphoreType.DMA(())],
        compiler_params=_params("arbitrary"),
        name="ffn_head",
    )(h, gain.reshape(1, D), wg, wu, wd, next_gain.reshape(1, D))


CONV_ROWS = 64


def _conv_piece(z_ref, w_ref, K, r0, lanes):
    zz = z_ref[pl.ds(r0, CONV_ROWS + SUBLANES), lanes]
    acc = zz[SUBLANES:, :] * w_ref[K - 1:K, lanes]
    for s in range(1, K):
        acc = acc + pltpu.roll(zz, s, axis=0)[SUBLANES:, :] * w_ref[K - 1 - s:K - s, lanes]
    return acc


def _load_halo(z_ref, halo_ref, j, first):
    @pl.when(first)
    def _():
        z_ref[0:SUBLANES, :] = jnp.zeros((SUBLANES, z_ref.shape[1]), F32)

    @pl.when(jnp.logical_not(first))
    def _():
        z_ref[0:SUBLANES, :] = halo_ref[j]


def _proj_gate_kernel(u_ref, wb_ref, wc_ref, wx_ref, cw_ref, *refs, tiles_per_seq, K, n_casts):
    cast_in, o_ref, cast_out, (z_ref, halo_ref) = refs[:n_casts], refs[n_casts], refs[n_casts + 1:-2], refs[-2:]
    i, j = pl.program_id(0), pl.program_id(1)
    tm, tn = o_ref.shape
    _load_halo(z_ref, halo_ref, j, (i % tiles_per_seq) == 0)
    _do_casts(cast_in, cast_out)
    u = u_ref[...]
    z_ref[SUBLANES:, :] = _dot_nt(u, wc_ref[...]) * _dot_nt(u, wx_ref[...])
    halo_ref[j] = z_ref[tm:tm + SUBLANES, :]
    b = _dot_nt(u, wb_ref[...])
    for r0 in range(0, tm, CONV_ROWS):
        for c0 in range(0, tn, LANES):
            lanes = slice(c0, c0 + LANES)
            y = b[r0:r0 + CONV_ROWS, lanes] * _conv_piece(z_ref, cw_ref, K, r0, lanes)
            o_ref[r0:r0 + CONV_ROWS, lanes] = y.astype(o_ref.dtype)


def _proj_gate(u, w_t, conv_w, *, width, seq_len, casts=()):
    M, D = u.shape
    K = conv_w.shape[0]
    tm = _tile(seq_len, 1024, CONV_ROWS)
    tn = _tile(width, MXU_COLS)
    nj = width // tn
    grid = (M // tm, nj)
    wspec = lambda off: pl.BlockSpec((tn, D), lambda i, j: (j + off, 0))
    casts, cast_specs, cast_out_specs, cast_shapes = _cast_specs(casts, grid)
    return pl.pallas_call(
        functools.partial(_proj_gate_kernel, tiles_per_seq=seq_len // tm, K=K, n_casts=len(casts)),
        grid=grid,
        in_specs=[pl.BlockSpec((tm, D), lambda i, j: (i, 0)), wspec(0), wspec(nj), wspec(2 * nj),
                  pl.BlockSpec((K, tn), lambda i, j: (0, j))] + cast_specs,
        out_specs=[pl.BlockSpec((tm, tn), lambda i, j: (i, j))] + cast_out_specs,
        out_shape=[jax.ShapeDtypeStruct((M, width), BF16)] + cast_shapes,
        scratch_shapes=[pltpu.VMEM((tm + SUBLANES, tn), F32), pltpu.VMEM((nj, SUBLANES, tn), F32)],
        compiler_params=_params("arbitrary", "arbitrary"),
        name="proj_gate",
    )(u, w_t, w_t, w_t, conv_w, *casts)


def _proj_conv_kernel(u_ref, w_ref, cw_ref, *refs, nj, tiles_per_seq, K, n_norm_tiles, n_conv_tiles, n_casts):
    cast_in, o_ref, cast_out, (z_ref, halo_ref) = refs[:n_casts], refs[n_casts], refs[n_casts + 1:-2], refs[-2:]
    s = pl.program_id(0)
    tm, tn = o_ref.shape

    @pl.when(s == 0)
    def _():
        z_ref[...] = jnp.zeros_like(z_ref)

    t = jnp.maximum(s - 1, 0)
    i, j = t // nj, t % nj
    _load_halo(z_ref, halo_ref, j, (i % tiles_per_seq) == 0)

    def step(conv, norm):
        for r0 in range(0, tm, CONV_ROWS):
            for c0 in range(0, tn, LANES):
                lanes = slice(c0, c0 + LANES)
                if conv:
                    x = _conv_piece(z_ref, cw_ref, K, r0, lanes)
                else:
                    x = z_ref[SUBLANES + r0:SUBLANES + r0 + CONV_ROWS, lanes]
                y = x * jax.nn.sigmoid(x)
                if norm:
                    y = y * lax.rsqrt(jnp.sum(y * y, axis=-1, keepdims=True) + L2_EPS)
                o_ref[r0:r0 + CONV_ROWS, lanes] = y.astype(o_ref.dtype)
        halo_ref[j] = z_ref[tm:tm + SUBLANES, :]
        z_ref[SUBLANES:, :] = _dot_nt(u_ref[...], w_ref[...])
        _do_casts(cast_in, cast_out)

    pl.when(j < n_norm_tiles)(lambda: step(True, True))
    pl.when((j >= n_norm_tiles) & (j < n_conv_tiles))(lambda: step(True, False))
    pl.when(j >= n_conv_tiles)(lambda: step(False, False))


def _proj_conv(u, w_t, conv_w, *, row0, n_cols, n_norm_cols, seq_len, casts=()):
    M, D = u.shape
    K, n_conv_cols = conv_w.shape
    N = n_cols
    tm = _tile(seq_len, 1024, CONV_ROWS)
    tn = _tile(N, 512)
    nj = N // tn
    steps = (M // tm) * nj
    assert row0 % tn == 0 and n_norm_cols % tn == 0 and n_conv_cols % tn == 0
    n_conv_tiles = n_conv_cols // tn
    off = row0 // tn
    cur = lambda s: jnp.minimum(s, steps - 1)
    prev = lambda s: jnp.maximum(s - 1, 0)
    casts, cast_specs, cast_out_specs, cast_shapes = _cast_specs(casts, (steps + 1,))
    return pl.pallas_call(
        functools.partial(_proj_conv_kernel, nj=nj, tiles_per_seq=seq_len // tm, K=K, n_norm_tiles=n_norm_cols // tn,
                          n_conv_tiles=n_conv_tiles, n_casts=len(casts)),
        grid=(steps + 1,),
        in_specs=[pl.BlockSpec((tm, D), lambda s: (cur(s) // nj, 0)),
                  pl.BlockSpec((tn, D), lambda s: (cur(s) % nj + off, 0)),
                  pl.BlockSpec((K, tn), lambda s: (0, jnp.minimum(prev(s) % nj, n_conv_tiles - 1)))] + cast_specs,
        out_specs=[pl.BlockSpec((tm, tn), lambda s: (prev(s) // nj, prev(s) % nj))] + cast_out_specs,
        out_shape=[jax.ShapeDtypeStruct((M, N), BF16)] + cast_shapes,
        scratch_shapes=[pltpu.VMEM((tm + SUBLANES, tn), F32), pltpu.VMEM((nj, SUBLANES, tn), F32)],
        compiler_params=_params("arbitrary"),
        name="proj_conv",
    )(u, w_t, conv_w, *casts)


def _gates_kernel(u_ref, w_ref, alog_ref, bias_ref, o_ref, *, n_heads):
    x = _dot_nt(w_ref[...], u_ref[...])
    s = x + bias_ref[...]
    softplus = jnp.maximum(s, 0.0) + jnp.log1p(jnp.exp(-jnp.abs(s)))
    g = -jnp.exp(alog_ref[...]) * softplus
    row = lax.broadcasted_iota(jnp.int32, x.shape, 0)
    o_ref[...] = jnp.where(row < n_heads, g, jax.nn.sigmoid(x))


def _gates(u, w_ab_t, a_log, dt_bias):
    M, D = u.shape
    H = a_log.shape[0]
    tm = _tile(M, 512)
    pad = jnp.zeros((H,), F32)
    col = lambda v: jnp.concatenate([v.astype(F32), pad]).reshape(2 * H, 1)
    return pl.pallas_call(
        functools.partial(_gates_kernel, n_heads=H),
        grid=(M // tm,),
        in_specs=[pl.BlockSpec((tm, D), lambda i: (i, 0)),
                  pl.BlockSpec((2 * H, D), lambda i: (0, 0)),
                  pl.BlockSpec((2 * H, 1), lambda i: (0, 0)),
                  pl.BlockSpec((2 * H, 1), lambda i: (0, 0))],
        out_specs=pl.BlockSpec((2 * H, tm), lambda i: (0, i)),
        out_shape=jax.ShapeDtypeStruct((2 * H, M), F32),
        compiler_params=_params("parallel"),
        name="gates",
    )(u, w_ab_t, col(a_log), col(dt_bias))


A_PITCH = CHUNK + SUBLANES


def _delta_kernel(q_ref, k_ref, v_ref, z_ref, g_ref, b_ref, gain_ref, o_ref,
                  gc_ref, s_ref, am_ref, at_ref, tm_ref, u_ref, wq_ref, kt_ref, a_ref, *, n_heads, n_chunks):
    C = CHUNK
    nc = n_heads * n_chunks
    scale = LANES ** -0.5
    row = lax.broadcasted_iota(jnp.int32, (C, C), 0)
    col = lax.broadcasted_iota(jnp.int32, (C, C), 1)
    eye = row == col
    causal = row >= col
    strict = row > col
    zeros_cc = jnp.zeros((C, C), F32)
    lane_row = lax.broadcasted_iota(jnp.int32, (LANES, LANES), 0)
    lane_col = lax.broadcasted_iota(jnp.int32, (LANES, LANES), 1)
    eye_lanes = jnp.where(lane_row == lane_col, 1.0, 0.0).astype(BF16)

    @pl.when(pl.program_id(2) == 0)
    def _():
        s_ref[...] = jnp.zeros_like(s_ref)

    prefix = jnp.where(row <= col, 1.0, 0.0)
    for hh in range(n_heads):
        gc_ref[hh] = _dot(g_ref[hh, 0], prefix, precision=HIGHEST)

    def to_col(r):
        return jnp.sum(jnp.where(eye, jnp.broadcast_to(r, (C, C)), 0.0), axis=1, keepdims=True)

    def chunk_slices(hh, c):
        return pl.ds(pl.multiple_of(c * C, C), C), slice(hh * LANES, (hh + 1) * LANES)

    def am_rows(hh, c):
        return pl.ds(pl.multiple_of((hh * n_chunks + c) * A_PITCH, SUBLANES), C)

    def intra(c, carry):
        for hh in range(n_heads):
            rows, lanes = chunk_slices(hh, c)
            k16 = k_ref[rows, lanes]
            k = k16.astype(F32)
            q = q_ref[rows, lanes].astype(F32) * scale
            g_row = gc_ref[hh, pl.ds(c, 1), :]
            g_col = to_col(g_row)
            b_col = to_col(b_ref[hh, 0, pl.ds(c, 1), :])
            decay = jnp.where(causal, jnp.exp(jnp.where(causal, g_col - g_row, 0.0)), 0.0)
            prod = _dot_nt(jnp.concatenate([k * b_col, q], axis=0).astype(BF16), k16)
            a = jnp.where(strict, prod[:C] * decay, 0.0)
            am_ref[am_rows(hh, c), :] = jnp.concatenate([a, zeros_cc], axis=1)
            a_ref[hh, rows, :] = (prod[C:] * decay).astype(BF16)
        return carry

    lax.fori_loop(0, n_chunks, intra, 0)

    for i in range(C):
        at_ref[i] = am_ref[pl.ds(i, nc, stride=A_PITCH), :].T[:C, :]

    zero_blk = jnp.zeros((SUBLANES, nc), F32)
    for i in range(C):
        nb = -(-i // SUBLANES)
        acc = [-at_ref[i, b * SUBLANES:(b + 1) * SUBLANES, :] for b in range(nb)]
        for j in range(1, i):
            a_ij = jnp.broadcast_to(at_ref[i, j:j + 1, :], (SUBLANES, nc))
            for b in range(-(-j // SUBLANES)):
                acc[b] = acc[b] - a_ij * tm_ref[j, b * SUBLANES:(b + 1) * SUBLANES, :]
        for b in range(C // SUBLANES):
            tm_ref[i, b * SUBLANES:(b + 1) * SUBLANES, :] = acc[b] if b < nb else zero_blk

    zeros_pad = jnp.zeros((LANES - C, nc), F32)
    for i in range(C):
        am_ref[pl.ds(i, nc, stride=A_PITCH), :] = jnp.concatenate([tm_ref[i], zeros_pad], axis=0).T

    def solve(c, carry):
        for hh in range(n_heads):
            rows, lanes = chunk_slices(hh, c)
            k = k_ref[rows, lanes].astype(F32)
            v = v_ref[rows, lanes].astype(F32)
            q = q_ref[rows, lanes].astype(F32) * scale
            g_row = gc_ref[hh, pl.ds(c, 1), :]
            g_col = to_col(g_row)
            b_col = to_col(b_ref[hh, 0, pl.ds(c, 1), :])
            e_col = jnp.exp(g_col)
            rhs = jnp.concatenate([v * b_col, k * (b_col * e_col)], axis=1)
            tm = am_ref[am_rows(hh, c), :][:, :C]
            sol = rhs + _dot(tm.astype(BF16), rhs.astype(BF16))
            u_ref[hh, rows, :] = sol[:, :LANES]
            wq_ref[hh, pl.ds(pl.multiple_of(c * 2 * C, 2 * C), C), :] = sol[:, LANES:].astype(BF16)
            wq_ref[hh, pl.ds(pl.multiple_of(c * 2 * C + C, C), C), :] = (q * e_col).astype(BF16)
            k_dec = k * jnp.exp(g_row[:, C - 1:C] - g_col)
            k_dec_t = _dot_nt(eye_lanes, k_dec.astype(BF16))
            kt_ref[hh, pl.ds(pl.multiple_of(c * LANES, LANES), LANES), :] = k_dec_t.astype(BF16)
        return carry

    lax.fori_loop(0, n_chunks, solve, 0)

    def recur(c, carry):
        heads = range(n_heads)
        rows = pl.ds(pl.multiple_of(c * C, C), C)
        S = [s_ref[hh] for hh in heads]
        r = [_dot(wq_ref[hh, pl.ds(pl.multiple_of(c * 2 * C, 2 * C), 2 * C), :], S[hh].astype(BF16))
             for hh in heads]
        v16 = [(u_ref[hh, rows, :] - r[hh][:C]).astype(BF16) for hh in heads]
        for hh in heads:
            g_last = gc_ref[hh, pl.ds(c, 1), :][:, C - 1:C]
            kt = kt_ref[hh, pl.ds(pl.multiple_of(c * LANES, LANES), LANES), :]
            s_ref[hh] = S[hh] * jnp.exp(g_last) + _dot(kt, v16[hh])
        for hh in heads:
            lanes = slice(hh * LANES, (hh + 1) * LANES)
            o = r[hh][C:] + _dot(a_ref[hh, rows, :], v16[hh])
            o_ref[rows, lanes] = (_rms(o, gain_ref[...]) * z_ref[rows, lanes].astype(F32)).astype(o_ref.dtype)
        return carry

    lax.fori_loop(0, n_chunks, recur, 0)


def _delta(qkvz, gb, gain, *, n_heads, seq_len):
    M = qkvz.shape[0]
    W = n_heads * LANES
    B = M // seq_len
    nh = min(n_heads, 8)
    ncb = LANES // nh
    tb = ncb * CHUNK
    hw = nh * LANES
    nb = W // hw
    nt = seq_len // tb
    assert LANES % nh == 0 and n_heads % nh == 0 and seq_len % tb == 0 and ncb % SUBLANES == 0
    tok = lambda off: pl.BlockSpec((tb, hw), lambda b, h, t: (b * nt + t, h + off))
    return pl.pallas_call(
        functools.partial(_delta_kernel, n_heads=nh, n_chunks=ncb),
        grid=(B, nb, nt),
        in_specs=[tok(0), tok(nb), tok(2 * nb), tok(3 * nb),
                  pl.BlockSpec((nh, 1, ncb, CHUNK), lambda b, h, t: (h, b, t, 0)),
                  pl.BlockSpec((nh, 1, ncb, CHUNK), lambda b, h, t: (h + nb, b, t, 0)),
                  pl.BlockSpec((1, LANES), lambda b, h, t: (0, 0))],
        out_specs=tok(0),
        out_shape=jax.ShapeDtypeStruct((M, W), BF16),
        scratch_shapes=[pltpu.VMEM((nh, ncb, CHUNK), F32),
                        pltpu.VMEM((nh, LANES, LANES), F32),
                        pltpu.VMEM((LANES * A_PITCH, LANES), F32),
                        pltpu.VMEM((CHUNK, CHUNK, LANES), F32),
                        pltpu.VMEM((CHUNK, CHUNK, LANES), F32),
                        pltpu.VMEM((nh, tb, LANES), F32),
                        pltpu.VMEM((nh, 2 * tb, LANES), BF16),
                        pltpu.VMEM((nh, ncb * LANES, CHUNK), BF16),
                        pltpu.VMEM((nh, tb, CHUNK), BF16)],
        compiler_params=_params("parallel", "parallel", "arbitrary"),
        name="delta_rule",
    )(qkvz, qkvz, qkvz, qkvz, gb, gb, gain.reshape(1, LANES))


def _merge_kernel(u_ref, yc_ref, yd_ref, wgc_ref, wgd_ref, wc_ref, wdn_ref, o_ref):
    u = u_ref[...]
    gc = jax.nn.sigmoid(_dot_nt(u, wgc_ref[...]))
    gd = jax.nn.sigmoid(_dot_nt(u, wgd_ref[...]))
    merged = gc * _dot(yc_ref[...], wc_ref[...]) + gd * _dot(yd_ref[...], wdn_ref[...])
    o_ref[...] = merged.astype(o_ref.dtype)


def _out_proj_kernel(m_ref, w_ref, h_ref, o_ref):
    o_ref[...] = h_ref[...] + _dot(m_ref[...], w_ref[...])


def _merge(u, y_conv, y_dn, h, wg_t, wc, wdn, wo):
    M, D = h.shape
    Wc = y_conv.shape[1]
    Wd = y_dn.shape[1]
    tm = _tile(M, 1024, SUBLANES)
    tn = _tile(D, MXU_COLS)
    nn = D // tn
    row = lambda i, j: (i, 0)
    colw = lambda i, j: (0, j)
    merged = pl.pallas_call(
        _merge_kernel,
        grid=(M // tm, nn),
        in_specs=[pl.BlockSpec((tm, D), row), _single((tm, Wc), row), _single((tm, Wd), row),
                  pl.BlockSpec((tn, D), lambda i, j: (j, 0)), pl.BlockSpec((tn, D), lambda i, j: (j + nn, 0)),
                  pl.BlockSpec((Wc, tn), colw), pl.BlockSpec((Wd, tn), colw)],
        out_specs=pl.BlockSpec((tm, tn), lambda i, j: (i, j)),
        out_shape=jax.ShapeDtypeStruct((M, D), BF16),
        compiler_params=_params("parallel", "arbitrary"),
        name="merge_gate",
    )(u, y_conv, y_dn, wg_t, wg_t, wc, wdn)
    to = _tile(D, 1024)
    return pl.pallas_call(
        _out_proj_kernel,
        grid=(M // tm, D // to),
        in_specs=[pl.BlockSpec((tm, D), row), pl.BlockSpec((D, to), colw),
                  pl.BlockSpec((tm, to), lambda i, j: (i, j))],
        out_specs=pl.BlockSpec((tm, to), lambda i, j: (i, j)),
        out_shape=jax.ShapeDtypeStruct((M, D), F32),
        compiler_params=_params("parallel", "arbitrary"),
        name="out_proj",
    )(merged, wo, h)


def kernel(x, ffn1_norm, ffn1_w_gate, ffn1_w_up, ffn1_w_down, mix_norm, w_in, conv_mixer_w, dn_conv_w, dn_a_log, dn_dt_bias, dn_out_norm, w_conv_branch, w_dn_branch, w_out, ffn2_norm, ffn2_w_gate, ffn2_w_up, ffn2_w_down, final_norm):
    B, T, D = x.shape
    M = B * T
    depth = ffn1_norm.shape[0]
    Wc = conv_mixer_w.shape[1]
    Wd = dn_conv_w.shape[1] // 3
    H = dn_a_log.shape[1]
    assert Wd == H * LANES and dn_out_norm.shape[1] == LANES and T % CHUNK == 0
    c_q = 3 * Wc
    c_z = c_q + 3 * Wd
    c_a = c_z + Wd
    c_gc = c_a + 2 * H
    c_gd = c_gc + D
    assert w_in.shape[2] == c_gd + D
    bf = lambda w: w.astype(BF16)

    h = x.reshape(M, D)
    for l in range(depth):
        last = l == depth - 1
        *head, wg1, wu1, wd1 = _ffn_head(h, ffn1_norm[l], ffn1_w_gate[l], ffn1_w_up[l], ffn1_w_down[l], mix_norm[l])
        w_in_t = w_in[l].T
        h, u, w_in16, w_gates16 = _ffn(h, ffn1_norm[l], wg1, wu1, wd1, mix_norm[l], final=False, head=head,
                                       casts=((w_in_t, 0, c_gc), (w_in_t, c_gc, 2 * D)))

        gb = _gates(u, w_in16[c_a:c_gc], dn_a_log[l], dn_dt_bias[l])
        y_conv, wc16, wdn16, wo16 = _proj_gate(u, w_in16, conv_mixer_w[l].T, width=Wc, seq_len=T,
                                               casts=(w_conv_branch[l], w_dn_branch[l], w_out[l]))
        qkvz, wg2, wu2, wd2 = _proj_conv(u, w_in16, dn_conv_w[l].T, row0=c_q, n_cols=4 * Wd, n_norm_cols=2 * Wd,
                                         seq_len=T, casts=(ffn2_w_gate[l], ffn2_w_up[l], ffn2_w_down[l]))
        y_dn = _delta(qkvz, gb.reshape(2 * H, B, T // CHUNK, CHUNK), dn_out_norm[l], n_heads=H, seq_len=T)
        h = _merge(u, y_conv, y_dn, h, w_gates16, wc16, wdn16, wo16)

        if last:
            h = _ffn(h, ffn2_norm[l], wg2, wu2, wd2, final_norm, final=True)
        else:
            h = _ffn(h, ffn2_norm[l], wg2, wu2, wd2, final_norm, final=False)[0]
    return h.reshape(B, T, D)
```

```python
import functools

import jax
import jax.numpy as jnp
from jax import lax
from jax.experimental import pallas as pl
from jax.experimental.pallas import tpu as pltpu

F32 = jnp.float32
BF16 = jnp.bfloat16
EPS = 1e-6
L2_EPS = 1e-6
CHUNK = 64
LANES = 128
SUBLANES = 8
MXU_COLS = 256
ROW_CHUNK = 32
FFN_ROWS = 512
FFN_TILES = (512, 256)
V7X_VMEM_LIMIT_BYTES = 56 * 1024 * 1024
HIGHEST = lax.Precision.HIGHEST


def _params(*sem):
    return pltpu.CompilerParams(dimension_semantics=sem, vmem_limit_bytes=V7X_VMEM_LIMIT_BYTES)


def _tile(n, pref, mult=LANES):
    if n <= pref:
        return n
    t = (pref // mult) * mult
    while t > 0 and n % t:
        t -= mult
    assert t > 0, (n, pref, mult)
    return t


def _single(block, index_map):
    return pl.BlockSpec(block, index_map, pipeline_mode=pl.Buffered(1))


def _rms(h, gain):
    return h * lax.rsqrt(jnp.mean(h * h, axis=-1, keepdims=True) + EPS) * gain


def _for_rows(n_rows, fn):
    def body(r, carry):
        fn(pl.ds(pl.multiple_of(r * ROW_CHUNK, ROW_CHUNK), ROW_CHUNK))
        return carry
    lax.fori_loop(0, n_rows // ROW_CHUNK, body, 0)


def _dot(a, b, **kw):
    return jnp.dot(a, b, preferred_element_type=F32, **kw)


def _dot_nt(a, b, **kw):
    return lax.dot_general(a, b, (((1,), (1,)), ((), ())), preferred_element_type=F32, **kw)


def _dot_tn(a, b, **kw):
    return lax.dot_general(a, b, (((0,), (0,)), ((), ())), preferred_element_type=F32, **kw)


def _spread(shape, grid, row0=0):
    R, Cn = shape
    gi, gj = (1,) * (2 - len(grid)) + tuple(grid)
    best = None
    for shift in range(12):
        bc = -(-Cn // (LANES << shift)) * LANES
        ncol = -(-Cn // bc)
        if ncol != 1 << shift:
            continue
        nr_max = (gi * gj) // ncol
        if nr_max == 0:
            break
        br = -(-R // (nr_max * 2 * SUBLANES)) * 2 * SUBLANES
        nr = -(-R // br)
        if row0 % br:
            continue
        key = (R % br == 0 and Cn % bc == 0 and 2 * nr * ncol >= gi * gj, nr * ncol)
        if best is None or key > best[0]:
            best = (key, br, bc, nr, shift)
    assert best is not None, f"no (16a, 128b) blocking of {shape} from row {row0} fits {gi * gj} steps"
    _, br, bc, nr, shift = best

    def index_map(*ids, offset=0):
        step = ids[0] if len(ids) == 1 else ids[0] * gj + ids[1]
        s = jnp.minimum(step, (nr << shift) - 1)
        return (s >> shift) + offset, s & ((1 << shift) - 1)
    return (br, bc), index_map, functools.partial(index_map, offset=row0 // br)


def _cast_specs(casts, grid):
    arrays, in_specs, out_specs, shapes = [], [], [], []
    for c in casts:
        a, row0, n_rows = c if isinstance(c, tuple) else (c, 0, c.shape[0])
        block, out_map, in_map = _spread((n_rows, a.shape[1]), grid, row0)
        arrays.append(a)
        in_specs.append(pl.BlockSpec(block, in_map))
        out_specs.append(pl.BlockSpec(block, out_map))
        shapes.append(jax.ShapeDtypeStruct((n_rows, a.shape[1]), BF16))
    return arrays, in_specs, out_specs, shapes


def _do_casts(src_refs, dst_refs):
    for src, dst in zip(src_refs, dst_refs):
        dst[...] = src[...].astype(BF16)


def _swiglu_tile(xn, wg, wu, wd, o_ref):
    n_cols = wg.shape[1]
    parts = [(c, min(c + MXU_COLS, n_cols)) for c in range(0, n_cols, MXU_COLS)]
    gate_up = [(_dot(xn, wg[:, a:b]), _dot(xn, wu[:, a:b])) for a, b in parts]
    for (a, b), (gate, up) in zip(parts, gate_up):
        act = (gate * jax.nn.sigmoid(gate) * (0.5 * up)).astype(BF16)
        o_ref[...] += _dot(act, wd[a:b, :])


def _ffn_kernel(h_ref, g_ref, wg_ref, wu_ref, wd_ref, ng_ref, *refs, final, n_casts, last_cols, has_head):
    cast_in, refs = refs[:n_casts], refs[n_casts:]
    if has_head:
        (head_o, head_u), refs = refs[:2], refs[2:]
    if final:
        o_ref, xn_ref = refs
        cast_out = ()
    elif has_head:
        o_ref, u_ref, *cast_out, xn_ref, sem = refs
    else:
        o_ref, u_ref, *cast_out, xn_ref = refs
    i, j = pl.program_id(0), pl.program_id(1)
    last = pl.num_programs(1) - 1
    active = (i > 0) if has_head else True

    if has_head:
        @pl.when((i == 0) & (j == 0))
        def _():
            copies = [pltpu.make_async_copy(head_o, o_ref, sem.at[0]), pltpu.make_async_copy(head_u, u_ref, sem.at[1])]
            for cp in copies:
                cp.start()
            for cp in copies:
                cp.wait()

        pl.when(i == 0)(lambda: _do_casts(cast_in, cast_out))

    @pl.when(active & (j == 0))
    def _():
        def init(rows):
            h = h_ref[rows, :]
            xn_ref[rows, :] = _rms(h, g_ref[...]).astype(BF16)
            o_ref[rows, :] = h
        _for_rows(h_ref.shape[0], init)

    def ff_tile(n_cols):
        _swiglu_tile(xn_ref[...], wg_ref.at[:, :n_cols], wu_ref.at[:, :n_cols], wd_ref.at[:n_cols, :], o_ref)
        _do_casts(cast_in, cast_out)

    if last_cols == wg_ref.shape[1]:
        pl.when(active)(lambda: ff_tile(last_cols))
    else:
        pl.when(active & (j < last))(lambda: ff_tile(wg_ref.shape[1]))
        pl.when(active & (j == last))(lambda: ff_tile(last_cols))

    @pl.when(active & (j == last))
    def _():
        def fin(rows):
            y = _rms(o_ref[rows, :], ng_ref[...])
            if final:
                o_ref[rows, :] = y
            else:
                u_ref[rows, :] = y.astype(BF16)
        _for_rows(h_ref.shape[0], fin)


def _ffn(h, gain, wg, wu, wd, next_gain, *, final, casts=(), head=None):
    M, D = h.shape
    F = wg.shape[1]
    tm = _tile(M, FFN_ROWS, ROW_CHUNK)
    has_head = head is not None

    def vmem_bytes(tf):
        steps = (M // tm) * -(-F // tf)
        rows = tm * D * (4 + 4 + 2 + (0 if final else 2))
        elems = [c[2] * c[0].shape[1] if isinstance(c, tuple) else c.size for c in casts]
        side = 0 if final else sum(2 * 6 * -(-n // steps) for n in elems)
        return rows + 2 * 3 * D * tf * 2 + side + 3 * tm * tf * 4

    tf = next((t for t in FFN_TILES if vmem_bytes(min(F, t)) <= V7X_VMEM_LIMIT_BYTES), FFN_TILES[-1])
    tf = min(F, tf)
    nj = -(-F // tf)
    last_cols = F - (nj - 1) * tf
    assert last_cols % LANES == 0
    grid = (M // tm, nj)
    row = lambda i, j: (i, 0)
    col = (lambda i, j: jnp.where(i == 0, 0, j)) if has_head else (lambda i, j: j)
    out_shape = [jax.ShapeDtypeStruct((M, D), F32)]
    out_specs = [_single((tm, D), row)]
    cast_specs, head_specs, scratch = [], [], [pltpu.VMEM((tm, D), BF16)]
    if not final:
        out_shape.append(jax.ShapeDtypeStruct((M, D), BF16))
        out_specs.append(_single((tm, D), row))
        casts, cast_specs, cast_out_specs, cast_shapes = _cast_specs(casts, grid)
        out_specs += cast_out_specs
        out_shape += cast_shapes
    if has_head:
        head_specs = [pl.BlockSpec(memory_space=pl.ANY)] * 2
        scratch.append(pltpu.SemaphoreType.DMA((2,)))
    res = pl.pallas_call(
        functools.partial(_ffn_kernel, final=final, n_casts=len(cast_specs), last_cols=last_cols, has_head=has_head),
        grid=grid,
        in_specs=[
            _single((tm, D), (lambda i, j: (jnp.maximum(i, 1), 0)) if has_head else row),
            pl.BlockSpec((1, D), lambda i, j: (0, 0)),
            pl.BlockSpec((D, tf), lambda i, j: (0, col(i, j))),
            pl.BlockSpec((D, tf), lambda i, j: (0, col(i, j))),
            pl.BlockSpec((tf, D), lambda i, j: (col(i, j), 0)),
            pl.BlockSpec((1, D), lambda i, j: (0, 0)),
        ] + cast_specs + head_specs,
        out_specs=out_specs,
        out_shape=out_shape,
        scratch_shapes=scratch,
        compiler_params=_params("parallel", "arbitrary"),
        name="ffn_final" if final else "ffn",
    )(h, gain.reshape(1, D), wg, wu, wd, next_gain.reshape(1, D), *casts, *(head or ()))
    return res[0] if final else res


def _ffn_head_kernel(h_hbm, g_ref, wg_ref, wu_ref, wd_ref, ng_ref, o_ref, u_ref, wg16_ref, wu16_ref, wd16_ref,
                     xn_ref, sem):
    j = pl.program_id(0)
    tm = o_ref.shape[0]

    @pl.when(j == 0)
    def _():
        cp = pltpu.make_async_copy(h_hbm.at[pl.ds(0, tm), :], o_ref, sem)
        cp.start()
        cp.wait()

        def init(rows):
            xn_ref[rows, :] = _rms(o_ref[rows, :], g_ref[...]).astype(BF16)
        _for_rows(tm, init)

    wg16_ref[...] = wg_ref[...].astype(BF16)
    wu16_ref[...] = wu_ref[...].astype(BF16)
    wd16_ref[...] = wd_ref[...].astype(BF16)
    _swiglu_tile(xn_ref[...], wg16_ref, wu16_ref, wd16_ref, o_ref)

    @pl.when(j == pl.num_programs(0) - 1)
    def _():
        def fin(rows):
            u_ref[rows, :] = _rms(o_ref[rows, :], ng_ref[...]).astype(BF16)
        _for_rows(tm, fin)


def _ffn_head(h, gain, wg, wu, wd, next_gain):
    M, D = h.shape
    F = wg.shape[1]
    tm = _tile(M, FFN_ROWS, ROW_CHUNK)
    tf = _tile(F, MXU_COLS)
    const = lambda j: (0, 0)
    wcol = pl.BlockSpec((D, tf), lambda j: (0, j))
    wrow = pl.BlockSpec((tf, D), lambda j: (j, 0))
    return pl.pallas_call(
        _ffn_head_kernel,
        grid=(F // tf,),
        in_specs=[pl.BlockSpec(memory_space=pl.ANY), pl.BlockSpec((1, D), const), wcol, wcol, wrow,
                  pl.BlockSpec((1, D), const)],
        out_specs=[_single((tm, D), const), _single((tm, D), const), wcol, wcol, wrow],
        out_shape=[jax.ShapeDtypeStruct((tm, D), F32), jax.ShapeDtypeStruct((tm, D), BF16),
                   jax.ShapeDtypeStruct(wg.shape, BF16), jax.ShapeDtypeStruct(wu.shape, BF16),
                   jax.ShapeDtypeStruct(wd.shape, BF16)],
        scratch_shapes=[pltpu.VMEM((tm, D), BF16), pltpu.SemaphoreType.DMA(())],
        compiler_params=_params("arbitrary"),
        name="ffn_head",
    )(h, gain.reshape(1, D), wg, wu, wd, next_gain.reshape(1, D))


CONV_ROWS = 64


def _conv_piece(z_ref, w_ref, K, r0, lanes):
    zz = z_ref[pl.ds(r0, CONV_ROWS + SUBLANES), lanes]
    acc = zz[SUBLANES:, :] * w_ref[K - 1:K, lanes]
    for s in range(1, K):
        acc = acc + pltpu.roll(zz, s, axis=0)[SUBLANES:, :] * w_ref[K - 1 - s:K - s, lanes]
    return acc


def _load_halo(z_ref, halo_ref, j, first):
    @pl.when(first)
    def _():
        z_ref[0:SUBLANES, :] = jnp.zeros((SUBLANES, z_ref.shape[1]), F32)

    @pl.when(jnp.logical_not(first))
    def _():
        z_ref[0:SUBLANES, :] = halo_ref[j]


def _decay_and_beta(w_ref, u_ref, alog_ref, bias_ref, n_heads):
    x = _dot_nt(w_ref[...], u_ref[...])
    s = x + bias_ref[...]
    softplus = jnp.maximum(s, 0.0) + jnp.log1p(jnp.exp(-jnp.abs(s)))
    g = -jnp.exp(alog_ref[...]) * softplus
    row = lax.broadcasted_iota(jnp.int32, x.shape, 0)
    return jnp.where(row < n_heads, g, jax.nn.sigmoid(x))


def _proj_gate_kernel(u_ref, wb_ref, wc_ref, wx_ref, cw_ref, wab_ref, alog_ref, bias_ref, *refs,
                      tiles_per_seq, K, n_casts, n_heads):
    cast_in, o_ref, gb_ref, cast_out = refs[:n_casts], refs[n_casts], refs[n_casts + 1], refs[n_casts + 2:-2]
    z_ref, halo_ref = refs[-2:]
    i, j = pl.program_id(0), pl.program_id(1)
    tm, tn = o_ref.shape

    @pl.when(j == 0)
    def _():
        gb_ref[...] = _decay_and_beta(wab_ref, u_ref, alog_ref, bias_ref, n_heads)

    _load_halo(z_ref, halo_ref, j, (i % tiles_per_seq) == 0)
    _do_casts(cast_in, cast_out)
    u = u_ref[...]
    z_ref[SUBLANES:, :] = _dot_nt(u, wc_ref[...]) * _dot_nt(u, wx_ref[...])
    halo_ref[j] = z_ref[tm:tm + SUBLANES, :]
    b = _dot_nt(u, wb_ref[...])
    for r0 in range(0, tm, CONV_ROWS):
        for c0 in range(0, tn, LANES):
            lanes = slice(c0, c0 + LANES)
            y = b[r0:r0 + CONV_ROWS, lanes] * _conv_piece(z_ref, cw_ref, K, r0, lanes)
            o_ref[r0:r0 + CONV_ROWS, lanes] = y.astype(o_ref.dtype)


def _proj_gate(u, w_t, conv_w, w_ab_t, a_log, dt_bias, *, width, seq_len, casts=()):
    M, D = u.shape
    K = conv_w.shape[0]
    H = a_log.shape[0]
    pad = jnp.zeros((H,), F32)
    col = lambda v: jnp.concatenate([v.astype(F32), pad]).reshape(2 * H, 1)
    const = lambda i, j: (0, 0)
    tm = _tile(seq_len, 1024, CONV_ROWS)
    tn = _tile(width, MXU_COLS)
    nj = width // tn
    grid = (M // tm, nj)
    wspec = lambda off: pl.BlockSpec((tn, D), lambda i, j: (j + off, 0))
    casts, cast_specs, cast_out_specs, cast_shapes = _cast_specs(casts, grid)
    return pl.pallas_call(
        functools.partial(_proj_gate_kernel, tiles_per_seq=seq_len // tm, K=K, n_casts=len(casts), n_heads=H),
        grid=grid,
        in_specs=[pl.BlockSpec((tm, D), lambda i, j: (i, 0)), wspec(0), wspec(nj), wspec(2 * nj),
                  pl.BlockSpec((K, tn), lambda i, j: (0, j)),
                  pl.BlockSpec((2 * H, D), const), pl.BlockSpec((2 * H, 1), const), pl.BlockSpec((2 * H, 1), const),
                  ] + cast_specs,
        out_specs=[pl.BlockSpec((tm, tn), lambda i, j: (i, j)),
                   pl.BlockSpec((2 * H, tm), lambda i, j: (0, i))] + cast_out_specs,
        out_shape=[jax.ShapeDtypeStruct((M, width), BF16), jax.ShapeDtypeStruct((2 * H, M), F32)] + cast_shapes,
        scratch_shapes=[pltpu.VMEM((tm + SUBLANES, tn), F32), pltpu.VMEM((nj, SUBLANES, tn), F32)],
        compiler_params=_params("arbitrary", "arbitrary"),
        name="proj_gate",
    )(u, w_t, w_t, w_t, conv_w, w_ab_t, col(a_log), col(dt_bias), *casts)


def _proj_conv_kernel(u_ref, w_ref, cw_ref, *refs, nj, tiles_per_seq, K, n_norm_tiles, n_conv_tiles, n_casts):
    cast_in, o_ref, cast_out, (z_ref, halo_ref) = refs[:n_casts], refs[n_casts], refs[n_casts + 1:-2], refs[-2:]
    s = pl.program_id(0)
    tm, tn = o_ref.shape

    @pl.when(s == 0)
    def _():
        z_ref[...] = jnp.zeros_like(z_ref)

    t = jnp.maximum(s - 1, 0)
    i, j = t // nj, t % nj
    _load_halo(z_ref, halo_ref, j, (i % tiles_per_seq) == 0)

    def step(conv, norm):
        for r0 in range(0, tm, CONV_ROWS):
            for c0 in range(0, tn, LANES):
                lanes = slice(c0, c0 + LANES)
                if conv:
                    x = _conv_piece(z_ref, cw_ref, K, r0, lanes)
                else:
                    x = z_ref[SUBLANES + r0:SUBLANES + r0 + CONV_ROWS, lanes]
                y = x * jax.nn.sigmoid(x)
                if norm:
                    y = y * lax.rsqrt(jnp.sum(y * y, axis=-1, keepdims=True) + L2_EPS)
                o_ref[r0:r0 + CONV_ROWS, lanes] = y.astype(o_ref.dtype)
        halo_ref[j] = z_ref[tm:tm + SUBLANES, :]
        z_ref[SUBLANES:, :] = _dot_nt(u_ref[...], w_ref[...])
        _do_casts(cast_in, cast_out)

    pl.when(j < n_norm_tiles)(lambda: step(True, True))
    pl.when((j >= n_norm_tiles) & (j < n_conv_tiles))(lambda: step(True, False))
    pl.when(j >= n_conv_tiles)(lambda: step(False, False))


def _proj_conv(u, w_t, conv_w, *, row0, n_cols, n_norm_cols, seq_len, casts=()):
    M, D = u.shape
    K, n_conv_cols = conv_w.shape
    N = n_cols
    tm = _tile(seq_len, 1024, CONV_ROWS)
    tn = _tile(N, 512)
    nj = N // tn
    steps = (M // tm) * nj
    assert row0 % tn == 0 and n_norm_cols % tn == 0 and n_conv_cols % tn == 0
    n_conv_tiles = n_conv_cols // tn
    off = row0 // tn
    cur = lambda s: jnp.minimum(s, steps - 1)
    prev = lambda s: jnp.maximum(s - 1, 0)
    casts, cast_specs, cast_out_specs, cast_shapes = _cast_specs(casts, (steps + 1,))
    return pl.pallas_call(
        functools.partial(_proj_conv_kernel, nj=nj, tiles_per_seq=seq_len // tm, K=K, n_norm_tiles=n_norm_cols // tn,
                          n_conv_tiles=n_conv_tiles, n_casts=len(casts)),
        grid=(steps + 1,),
        in_specs=[pl.BlockSpec((tm, D), lambda s: (cur(s) // nj, 0)),
                  pl.BlockSpec((tn, D), lambda s: (cur(s) % nj + off, 0)),
                  pl.BlockSpec((K, tn), lambda s: (0, jnp.minimum(prev(s) % nj, n_conv_tiles - 1)))] + cast_specs,
        out_specs=[pl.BlockSpec((tm, tn), lambda s: (prev(s) // nj, prev(s) % nj))] + cast_out_specs,
        out_shape=[jax.ShapeDtypeStruct((M, N), BF16)] + cast_shapes,
        scratch_shapes=[pltpu.VMEM((tm + SUBLANES, tn), F32), pltpu.VMEM((nj, SUBLANES, tn), F32)],
        compiler_params=_params("arbitrary"),
        name="proj_conv",
    )(u, w_t, conv_w, *casts)


A_PITCH = CHUNK + SUBLANES


def _delta_kernel(q_ref, k_ref, v_ref, z_ref, g_ref, b_ref, gain_ref, o_ref,
                  gc_ref, s_ref, am_ref, at_ref, tm_ref, u_ref, wq_ref, kt_ref, a_ref, *, n_heads, n_chunks):
    C = CHUNK
    nc = n_heads * n_chunks
    scale = LANES ** -0.5
    row = lax.broadcasted_iota(jnp.int32, (C, C), 0)
    col = lax.broadcasted_iota(jnp.int32, (C, C), 1)
    eye = row == col
    causal = row >= col
    strict = row > col
    zeros_cc = jnp.zeros((C, C), F32)
    lane_row = lax.broadcasted_iota(jnp.int32, (LANES, LANES), 0)
    lane_col = lax.broadcasted_iota(jnp.int32, (LANES, LANES), 1)
    eye_lanes = jnp.where(lane_row == lane_col, 1.0, 0.0).astype(BF16)

    @pl.when(pl.program_id(2) == 0)
    def _():
        s_ref[...] = jnp.zeros_like(s_ref)

    prefix = jnp.where(row <= col, 1.0, 0.0)
    for hh in range(n_heads):
        gc_ref[hh] = _dot(g_ref[hh, 0], prefix, precision=HIGHEST)

    def to_col(r):
        return jnp.sum(jnp.where(eye, jnp.broadcast_to(r, (C, C)), 0.0), axis=1, keepdims=True)

    def chunk_slices(hh, c):
        return pl.ds(pl.multiple_of(c * C, C), C), slice(hh * LANES, (hh + 1) * LANES)

    def am_rows(hh, c):
        return pl.ds(pl.multiple_of((hh * n_chunks + c) * A_PITCH, SUBLANES), C)

    def intra(c, carry):
        for hh in range(n_heads):
            rows, lanes = chunk_slices(hh, c)
            k16 = k_ref[rows, lanes]
            k = k16.astype(F32)
            q = q_ref[rows, lanes].astype(F32) * scale
            g_row = gc_ref[hh, pl.ds(c, 1), :]
            g_col = to_col(g_row)
            b_col = to_col(b_ref[hh, 0, pl.ds(c, 1), :])
            decay = jnp.where(causal, jnp.exp(jnp.where(causal, g_col - g_row, 0.0)), 0.0)
            prod = _dot_nt(jnp.concatenate([k * b_col, q], axis=0).astype(BF16), k16)
            a = jnp.where(strict, prod[:C] * decay, 0.0)
            am_ref[am_rows(hh, c), :] = jnp.concatenate([a, zeros_cc], axis=1)
            a_ref[hh, rows, :] = (prod[C:] * decay).astype(BF16)
        return carry

    lax.fori_loop(0, n_chunks, intra, 0)

    for i in range(C):
        at_ref[i] = am_ref[pl.ds(i, nc, stride=A_PITCH), :].T[:C, :]

    zero_blk = jnp.zeros((SUBLANES, nc), F32)
    for i in range(C):
        nb = -(-i // SUBLANES)
        acc = [-at_ref[i, b * SUBLANES:(b + 1) * SUBLANES, :] for b in range(nb)]
        for j in range(1, i):
            a_ij = jnp.broadcast_to(at_ref[i, j:j + 1, :], (SUBLANES, nc))
            for b in range(-(-j // SUBLANES)):
                acc[b] = acc[b] - a_ij * tm_ref[j, b * SUBLANES:(b + 1) * SUBLANES, :]
        for b in range(C // SUBLANES):
            tm_ref[i, b * SUBLANES:(b + 1) * SUBLANES, :] = acc[b] if b < nb else zero_blk

    zeros_pad = jnp.zeros((LANES - C, nc), F32)
    for i in range(C):
        am_ref[pl.ds(i, nc, stride=A_PITCH), :] = jnp.concatenate([tm_ref[i], zeros_pad], axis=0).T

    def solve(c, carry):
        for hh in range(n_heads):
            rows, lanes = chunk_slices(hh, c)
            k = k_ref[rows, lanes].astype(F32)
            v = v_ref[rows, lanes].astype(F32)
            q = q_ref[rows, lanes].astype(F32) * scale
            g_row = gc_ref[hh, pl.ds(c, 1), :]
            g_col = to_col(g_row)
            b_col = to_col(b_ref[hh, 0, pl.ds(c, 1), :])
            e_col = jnp.exp(g_col)
            rhs = jnp.concatenate([v * b_col, k * (b_col * e_col)], axis=1)
            tm = am_ref[am_rows(hh, c), :][:, :C]
            sol = rhs + _dot(tm.astype(BF16), rhs.astype(BF16))
            u_ref[hh, rows, :] = sol[:, :LANES]
            wq_ref[hh, pl.ds(pl.multiple_of(c * 2 * C, 2 * C), C), :] = sol[:, LANES:].astype(BF16)
            wq_ref[hh, pl.ds(pl.multiple_of(c * 2 * C + C, C), C), :] = (q * e_col).astype(BF16)
            k_dec = k * jnp.exp(g_row[:, C - 1:C] - g_col)
            k_dec_t = _dot_nt(eye_lanes, k_dec.astype(BF16))
            kt_ref[hh, pl.ds(pl.multiple_of(c * LANES, LANES), LANES), :] = k_dec_t.astype(BF16)
        return carry

    lax.fori_loop(0, n_chunks, solve, 0)

    def recur(c, carry):
        heads = range(n_heads)
        rows = pl.ds(pl.multiple_of(c * C, C), C)
        S = [s_ref[hh] for hh in heads]
        r = [_dot(wq_ref[hh, pl.ds(pl.multiple_of(c * 2 * C, 2 * C), 2 * C), :], S[hh].astype(BF16))
             for hh in heads]
        v16 = [(u_ref[hh, rows, :] - r[hh][:C]).astype(BF16) for hh in heads]
        for hh in heads:
            g_last = gc_ref[hh, pl.ds(c, 1), :][:, C - 1:C]
            kt = kt_ref[hh, pl.ds(pl.multiple_of(c * LANES, LANES), LANES), :]
            s_ref[hh] = S[hh] * jnp.exp(g_last) + _dot(kt, v16[hh])
        for hh in heads:
            lanes = slice(hh * LANES, (hh + 1) * LANES)
            o = r[hh][C:] + _dot(a_ref[hh, rows, :], v16[hh])
            o_ref[rows, lanes] = (_rms(o, gain_ref[...]) * z_ref[rows, lanes].astype(F32)).astype(o_ref.dtype)
        return carry

    lax.fori_loop(0, n_chunks, recur, 0)


def _delta(qkvz, gb, gain, *, n_heads, seq_len):
    M = qkvz.shape[0]
    W = n_heads * LANES
    B = M // seq_len
    nh = min(n_heads, 8)
    ncb = LANES // nh
    tb = ncb * CHUNK
    hw = nh * LANES
    nb = W // hw
    nt = seq_len // tb
    assert LANES % nh == 0 and n_heads % nh == 0 and seq_len % tb == 0 and ncb % SUBLANES == 0
    tok = lambda off: pl.BlockSpec((tb, hw), lambda b, h, t: (b * nt + t, h + off))
    return pl.pallas_call(
        functools.partial(_delta_kernel, n_heads=nh, n_chunks=ncb),
        grid=(B, nb, nt),
        in_specs=[tok(0), tok(nb), tok(2 * nb), tok(3 * nb),
                  pl.BlockSpec((nh, 1, ncb, CHUNK), lambda b, h, t: (h, b, t, 0)),
                  pl.BlockSpec((nh, 1, ncb, CHUNK), lambda b, h, t: (h + nb, b, t, 0)),
                  pl.BlockSpec((1, LANES), lambda b, h, t: (0, 0))],
        out_specs=tok(0),
        out_shape=jax.ShapeDtypeStruct((M, W), BF16),
        scratch_shapes=[pltpu.VMEM((nh, ncb, CHUNK), F32),
                        pltpu.VMEM((nh, LANES, LANES), F32),
                        pltpu.VMEM((LANES * A_PITCH, LANES), F32),
                        pltpu.VMEM((CHUNK, CHUNK, LANES), F32),
                        pltpu.VMEM((CHUNK, CHUNK, LANES), F32),
                        pltpu.VMEM((nh, tb, LANES), F32),
                        pltpu.VMEM((nh, 2 * tb, LANES), BF16),
                        pltpu.VMEM((nh, ncb * LANES, CHUNK), BF16),
                        pltpu.VMEM((nh, tb, CHUNK), BF16)],
        compiler_params=_params("parallel", "parallel", "arbitrary"),
        name="delta_rule",
    )(qkvz, qkvz, qkvz, qkvz, gb, gb, gain.reshape(1, LANES))


def _merge_kernel(u_ref, yc_ref, yd_ref, wgc_ref, wgd_ref, wc_ref, wdn_ref, o_ref):
    u = u_ref[...]
    gc = jax.nn.sigmoid(_dot_nt(u, wgc_ref[...]))
    gd = jax.nn.sigmoid(_dot_nt(u, wgd_ref[...]))
    merged = gc * _dot(yc_ref[...], wc_ref[...]) + gd * _dot(yd_ref[...], wdn_ref[...])
    o_ref[...] = merged.astype(o_ref.dtype)


def _out_proj_kernel(m_ref, w_ref, h_ref, o_ref):
    o_ref[...] = h_ref[...] + _dot(m_ref[...], w_ref[...])


def _merge(u, y_conv, y_dn, h, wg_t, wc, wdn, wo):
    M, D = h.shape
    Wc = y_conv.shape[1]
    Wd = y_dn.shape[1]
    tm = _tile(M, 1024, SUBLANES)
    tn = _tile(D, MXU_COLS)
    nn = D // tn
    row = lambda i, j: (i, 0)
    colw = lambda i, j: (0, j)
    merged = pl.pallas_call(
        _merge_kernel,
        grid=(M // tm, nn),
        in_specs=[pl.BlockSpec((tm, D), row), _single((tm, Wc), row), _single((tm, Wd), row),
                  pl.BlockSpec((tn, D), lambda i, j: (j, 0)), pl.BlockSpec((tn, D), lambda i, j: (j + nn, 0)),
                  pl.BlockSpec((Wc, tn), colw), pl.BlockSpec((Wd, tn), colw)],
        out_specs=pl.BlockSpec((tm, tn), lambda i, j: (i, j)),
        out_shape=jax.ShapeDtypeStruct((M, D), BF16),
        compiler_params=_params("parallel", "arbitrary"),
        name="merge_gate",
    )(u, y_conv, y_dn, wg_t, wg_t, wc, wdn)
    to = _tile(D, 1024)
    return pl.pallas_call(
        _out_proj_kernel,
        grid=(M // tm, D // to),
        in_specs=[pl.BlockSpec((tm, D), row), pl.BlockSpec((D, to), colw),
                  pl.BlockSpec((tm, to), lambda i, j: (i, j))],
        out_specs=pl.BlockSpec((tm, to), lambda i, j: (i, j)),
        out_shape=jax.ShapeDtypeStruct((M, D), F32),
        compiler_params=_params("parallel", "arbitrary"),
        name="out_proj",
    )(merged, wo, h)


def kernel(x, ffn1_norm, ffn1_w_gate, ffn1_w_up, ffn1_w_down, mix_norm, w_in, conv_mixer_w, dn_conv_w, dn_a_log, dn_dt_bias, dn_out_norm, w_conv_branch, w_dn_branch, w_out, ffn2_norm, ffn2_w_gate, ffn2_w_up, ffn2_w_down, final_norm):
    B, T, D = x.shape
    M = B * T
    depth = ffn1_norm.shape[0]
    Wc = conv_mixer_w.shape[1]
    Wd = dn_conv_w.shape[1] // 3
    H = dn_a_log.shape[1]
    assert Wd == H * LANES and dn_out_norm.shape[1] == LANES and T % CHUNK == 0
    c_q = 3 * Wc
    c_z = c_q + 3 * Wd
    c_a = c_z + Wd
    c_gc = c_a + 2 * H
    c_gd = c_gc + D
    assert w_in.shape[2] == c_gd + D

    h = x.reshape(M, D)
    for l in range(depth):
        last = l == depth - 1
        *head, wg1, wu1, wd1 = _ffn_head(h, ffn1_norm[l], ffn1_w_gate[l], ffn1_w_up[l], ffn1_w_down[l], mix_norm[l])
        w_in_t = w_in[l].T
        h, u, w_in16, w_gates16 = _ffn(h, ffn1_norm[l], wg1, wu1, wd1, mix_norm[l], final=False, head=head,
                                       casts=((w_in_t, 0, c_gc), (w_in_t, c_gc, 2 * D)))

        y_conv, gb, wc16, wdn16, wo16 = _proj_gate(u, w_in16, conv_mixer_w[l].T, w_in16[c_a:c_gc], dn_a_log[l],
                                                   dn_dt_bias[l], width=Wc, seq_len=T,
                                                   casts=(w_conv_branch[l], w_dn_branch[l], w_out[l]))
        qkvz, wg2, wu2, wd2 = _proj_conv(u, w_in16, dn_conv_w[l].T, row0=c_q, n_cols=4 * Wd, n_norm_cols=2 * Wd,
                                         seq_len=T, casts=(ffn2_w_gate[l], ffn2_w_up[l], ffn2_w_down[l]))
        y_dn = _delta(qkvz, gb.reshape(2 * H, B, T // CHUNK, CHUNK), dn_out_norm[l], n_heads=H, seq_len=T)
        h = _merge(u, y_conv, y_dn, h, w_gates16, wc16, wdn16, wo16)

        if last:
            h = _ffn(h, ffn2_norm[l], wg2, wu2, wd2, final_norm, final=True)
        else:
            h = _ffn(h, ffn2_norm[l], wg2, wu2, wd2, final_norm, final=False)[0]
    return h.reshape(B, T, D)
```
